```python
import jax, jax.numpy as jnp
from jax import lax
import numpy as np

D_MODEL = 1024
BATCH = 16
SEQ = 2048
DEPTH = 1

RET_HEADS = 4
RET_QK_DIM = D_MODEL // RET_HEADS
RET_V_DIM = 2 * RET_QK_DIM
RET_QK_WIDTH = RET_HEADS * RET_QK_DIM
RET_V_WIDTH = RET_HEADS * RET_V_DIM
RET_CHUNK = 128
ROPE_BASE = 10000.0
LRU_WIDTH = D_MODEL
LRU_BLOCK = 256
LRU_BLOCKS = LRU_WIDTH // LRU_BLOCK
LRU_CONV = 4
LRU_C = 8.0
FFN_HIDDEN = ((8 * D_MODEL // 3 + 255) // 256) * 256
FFN_CONV = 3
NORM_EPS = 1e-6
GN_EPS = 1e-5

IN_SIZES = (RET_QK_WIDTH, RET_QK_WIDTH, RET_V_WIDTH, RET_V_WIDTH, LRU_WIDTH, LRU_WIDTH, D_MODEL, D_MODEL)
IN_WIDTH = sum(IN_SIZES)
SPLIT_POINTS = [int(v) for v in np.cumsum(IN_SIZES)[:-1]]

kernel_name = "hybrid_retention_rglru_convffn"


def rms_norm(x, g):
    xf = x.astype(jnp.float32)
    y = xf * lax.rsqrt(jnp.mean(jnp.square(xf), axis=-1, keepdims=True) + NORM_EPS)
    return (y * g.astype(jnp.float32)).astype(x.dtype)


def head_norm(o, g):
    b, s = o.shape[0], o.shape[1]
    of = o.astype(jnp.float32)
    mu = jnp.mean(of, axis=-1, keepdims=True)
    var = jnp.mean(jnp.square(of - mu), axis=-1, keepdims=True)
    y = ((of - mu) * lax.rsqrt(var + GN_EPS)).reshape(b, s, -1)
    return (y * g.astype(jnp.float32)).astype(o.dtype)


def rotary(t, cos, sin):
    t1, t2 = jnp.split(t, 2, axis=-1)
    c = cos[None, :, None, :]
    s = sin[None, :, None, :]
    return jnp.concatenate([t1 * c - t2 * s, t1 * s + t2 * c], axis=-1)


def causal_depthwise_conv(x, w, b):
    width, ch = w.shape
    y = lax.conv_general_dilated(x, w[:, None, :].astype(x.dtype), window_strides=(1,),
                                 padding=((width - 1, 0),), dimension_numbers=('NWC', 'WIO', 'NWC'),
                                 feature_group_count=ch)
    return y + b.astype(x.dtype)


def retention(q, k, v):
    b, s, h, dk = q.shape
    dv = v.shape[-1]
    nc = s // RET_CHUNK
    dt = q.dtype

    def chunks(t):
        return t.reshape(b, nc, RET_CHUNK, h, t.shape[-1]).transpose(1, 0, 3, 2, 4)

    log_g = jnp.log1p(-jnp.exp2(-5.0 - jnp.arange(h, dtype=jnp.float32)))
    pos = jnp.arange(RET_CHUNK, dtype=jnp.float32)
    rel = pos[:, None] - pos[None, :]
    decay_in = jnp.where(rel >= 0, jnp.exp(log_g[:, None, None] * jnp.maximum(rel, 0.0)), 0.0).astype(dt)
    decay_q = jnp.exp(log_g[:, None] * (pos + 1.0)).astype(dt)
    decay_k = jnp.exp(log_g[:, None] * (RET_CHUNK - 1.0 - pos)).astype(dt)
    decay_chunk = jnp.exp(log_g * RET_CHUNK).astype(dt)

    def step(state, qkv):
        qi, ki, vi = qkv
        scores = jnp.einsum('bhnd,bhmd->bhnm', qi, ki) * decay_in
        o = (jnp.einsum('bhnm,bhmv->bhnv', scores, vi)
             + jnp.einsum('bhnd,bhdv->bhnv', qi, state) * decay_q[:, :, None])
        state = (state * decay_chunk[:, None, None]
                 + jnp.einsum('bhmd,bhmv->bhdv', ki * decay_k[:, :, None], vi))
        return state, o

    init = jnp.zeros((b, h, dk, dv), dt)
    _, o = lax.scan(step, init, (chunks(q), chunks(k), chunks(v)))
    return o.transpose(1, 0, 3, 2, 4).reshape(b, s, h, dv)


def rg_lru(x, w_a, b_a, w_i, b_i, a_param):
    b, s, _ = x.shape
    xb = x.reshape(b, s, LRU_BLOCKS, LRU_BLOCK)
    r = jax.nn.sigmoid((jnp.einsum('bsnc,ncd->bsnd', xb, w_a) + b_a).reshape(b, s, LRU_WIDTH).astype(jnp.float32))
    i = jax.nn.sigmoid((jnp.einsum('bsnc,ncd->bsnd', xb, w_i) + b_i).reshape(b, s, LRU_WIDTH).astype(jnp.float32))
    log_a = -LRU_C * r * jax.nn.softplus(-a_param.astype(jnp.float32))
    a = jnp.exp(log_a)
    u = jnp.sqrt(-jnp.expm1(2.0 * log_a)) * i * x.astype(jnp.float32)

    def combine(left, right):
        return (left[0] * right[0], right[0] * left[1] + right[1])

    _, hs = lax.associative_scan(combine, (a, u), axis=1)
    return hs.astype(x.dtype)


def setup_inputs(seed: int = 0) -> dict:
    key = jax.random.key(seed)
    ks = jax.random.split(key, 24)
    f = jnp.float32

    def nrm(k, shape, scale):
        return jax.random.normal(k, shape, f) * scale

    u = jax.random.uniform(ks[12], (DEPTH, LRU_WIDTH), f, 0.9, 0.999)
    a0 = u ** (1.0 / LRU_C)
    return {
        "x": nrm(ks[0], (BATCH, SEQ, D_MODEL), 1.0),
        "in_norm_g": 1.0 + nrm(ks[1], (DEPTH, D_MODEL), 0.02),
        "w_in": nrm(ks[2], (DEPTH, D_MODEL, IN_WIDTH), D_MODEL ** -0.5),
        "ret_gn_g": 1.0 + nrm(ks[3], (DEPTH, RET_V_WIDTH), 0.02),
        "w_ret_o": nrm(ks[4], (DEPTH, RET_V_WIDTH, D_MODEL), RET_V_WIDTH ** -0.5),
        "lru_conv_w": nrm(ks[5], (DEPTH, LRU_CONV, LRU_WIDTH), LRU_CONV ** -0.5),
        "lru_conv_b": nrm(ks[6], (DEPTH, LRU_WIDTH), 0.02),
        "lru_w_a": nrm(ks[7], (DEPTH, LRU_BLOCKS, LRU_BLOCK, LRU_BLOCK), LRU_BLOCK ** -0.5),
        "lru_b_a": nrm(ks[8], (DEPTH, LRU_BLOCKS, LRU_BLOCK), 0.02),
        "lru_w_i": nrm(ks[9], (DEPTH, LRU_BLOCKS, LRU_BLOCK, LRU_BLOCK), LRU_BLOCK ** -0.5),
        "lru_b_i": nrm(ks[10], (DEPTH, LRU_BLOCKS, LRU_BLOCK), 0.02),
        "lru_a_param": jnp.log(a0) - jnp.log1p(-a0),
        "w_rnn_o": nrm(ks[13], (DEPTH, LRU_WIDTH, D_MODEL), LRU_WIDTH ** -0.5),
        "w_out": nrm(ks[14], (DEPTH, D_MODEL, D_MODEL), D_MODEL ** -0.5),
        "ffn_norm_g": 1.0 + nrm(ks[15], (DEPTH, D_MODEL), 0.02),
        "w_up": nrm(ks[16], (DEPTH, D_MODEL, 2 * FFN_HIDDEN), D_MODEL ** -0.5),
        "ffn_conv_w": nrm(ks[17], (DEPTH, FFN_CONV, 2 * FFN_HIDDEN), FFN_CONV ** -0.5),
        "ffn_conv_b": nrm(ks[18], (DEPTH, 2 * FFN_HIDDEN), 0.02),
        "w_down": nrm(ks[19], (DEPTH, FFN_HIDDEN, D_MODEL), FFN_HIDDEN ** -0.5),
        "final_norm_g": 1.0 + nrm(ks[20], (D_MODEL,), 0.02),
    }


def reference(x, in_norm_g, w_in, ret_gn_g, w_ret_o, lru_conv_w, lru_conv_b, lru_w_a, lru_b_a,
              lru_w_i, lru_b_i, lru_a_param, w_rnn_o, w_out, ffn_norm_g, w_up, ffn_conv_w,
              ffn_conv_b, w_down, final_norm_g):
    b, s, _ = x.shape
    pos = jnp.arange(s, dtype=jnp.float32)
    inv_freq = ROPE_BASE ** (-jnp.arange(0, RET_QK_DIM, 2, dtype=jnp.float32) / RET_QK_DIM)
    ang = pos[:, None] * inv_freq[None, :]
    cos = jnp.cos(ang).astype(x.dtype)
    sin = jnp.sin(ang).astype(x.dtype)

    for l in range(DEPTH):
        h = rms_norm(x, in_norm_g[l])
        proj = h @ w_in[l]
        q, k, v, g, xr, gr, m_ret, m_rnn = jnp.split(proj, SPLIT_POINTS, axis=-1)

        q = rotary(q.reshape(b, s, RET_HEADS, RET_QK_DIM), cos, sin)
        k = rotary(k.reshape(b, s, RET_HEADS, RET_QK_DIM), cos, sin) * (RET_QK_DIM ** -0.5)
        v = v.reshape(b, s, RET_HEADS, RET_V_DIM)
        o = head_norm(retention(q, k, v), ret_gn_g[l])
        y_ret = (jax.nn.silu(g) * o) @ w_ret_o[l]

        xr = causal_depthwise_conv(xr, lru_conv_w[l], lru_conv_b[l])
        hr = rg_lru(xr, lru_w_a[l], lru_b_a[l], lru_w_i[l], lru_b_i[l], lru_a_param[l])
        y_rnn = (hr * jax.nn.gelu(gr)) @ w_rnn_o[l]

        mixed = jax.nn.sigmoid(m_ret) * y_ret + jax.nn.sigmoid(m_rnn) * y_rnn
        x = x + mixed @ w_out[l]

        h = rms_norm(x, ffn_norm_g[l])
        up = causal_depthwise_conv(h @ w_up[l], ffn_conv_w[l], ffn_conv_b[l])
        gate, val = jnp.split(up, 2, axis=-1)
        x = x + (jax.nn.silu(gate) * val) @ w_down[l]

    return rms_norm(x, final_norm_g)
```

```python
import functools

import jax
import jax.numpy as jnp
from jax import lax
from jax.experimental import pallas as pl
from jax.experimental.pallas import tpu as pltpu

F32 = jnp.float32
BF16 = jnp.bfloat16

D_MODEL = 1024
RET_HEADS = 4
RET_QK_DIM = 256
RET_V_DIM = 512
RET_QK_WIDTH = RET_HEADS * RET_QK_DIM
RET_V_WIDTH = RET_HEADS * RET_V_DIM
RET_CHUNK = 128
ROPE_BASE = 10000.0
LRU_WIDTH = 1024
LRU_BLOCK = 256
LRU_BLOCKS = 4
LRU_CONV = 4
LRU_C = 8.0
FFN_HIDDEN = 2816
FFN_CONV = 3
NORM_EPS = 1e-6
GN_EPS = 1e-5

SUBLANES = 8
LANES = 128
V7X_VMEM_LIMIT = 56 * 1024 * 1024

COL_Q = 0
COL_K = COL_Q + RET_QK_WIDTH
COL_V = COL_K + RET_QK_WIDTH
COL_G = COL_V + RET_V_WIDTH
COL_XR = COL_G + RET_V_WIDTH
COL_GR = COL_XR + LRU_WIDTH
COL_MRET = COL_GR + LRU_WIDTH
COL_MRNN = COL_MRET + D_MODEL
IN_WIDTH = COL_MRNN + D_MODEL

INPROJ_ROWS = 512
MIXER_ROWS = 256
FFN_ROWS = 256
FFN_SPLITS = ((0, 1536), (1536, 1280))


def _const_spec(shape):
    nd = len(shape)
    return pl.BlockSpec(shape, lambda *_: (0,) * nd, pipeline_mode=pl.Buffered(1))


def _rms_norm(x, g):
    ms = jnp.mean(x * x, axis=-1, keepdims=True)
    return x * lax.rsqrt(ms + NORM_EPS) * g


def _shift_rows(cat, j, rows):
    return pltpu.roll(cat, j, axis=0)[SUBLANES:SUBLANES + rows, :]


def _inproj_kernel(x_ref, g_ref, w_ref, cos_ref, sin_ref,
                   q_ref, k_ref, v_ref, sg_ref, xr_ref, gg_ref, sr_ref, sn_ref):
    h = _rms_norm(x_ref[...], g_ref[...]).astype(BF16)
    cos = cos_ref[...]
    sin = sin_ref[...]

    def proj(col, width):
        return jnp.dot(h, w_ref[:, col:col + width], preferred_element_type=F32)

    def rotary_store(acc, out_ref, scale):
        half = RET_QK_DIM // 2
        for hd in range(RET_HEADS):
            lo = hd * RET_QK_DIM
            t1 = acc[:, lo:lo + half]
            t2 = acc[:, lo + half:lo + RET_QK_DIM]
            out_ref[:, lo:lo + half] = ((t1 * cos - t2 * sin) * scale).astype(BF16)
            out_ref[:, lo + half:lo + RET_QK_DIM] = ((t1 * sin + t2 * cos) * scale).astype(BF16)

    rotary_store(proj(COL_Q, RET_QK_WIDTH), q_ref, 1.0)
    rotary_store(proj(COL_K, RET_QK_WIDTH), k_ref, RET_QK_DIM ** -0.5)
    step = 1024
    for c in range(0, RET_V_WIDTH, step):
        v_ref[:, c:c + step] = proj(COL_V + c, step).astype(BF16)
    for c in range(0, RET_V_WIDTH, step):
        sg_ref[:, c:c + step] = jax.nn.silu(proj(COL_G + c, step)).astype(BF16)
    xr_ref[...] = proj(COL_XR, LRU_WIDTH).astype(BF16)
    gg_ref[...] = jax.nn.gelu(proj(COL_GR, LRU_WIDTH)).astype(BF16)
    sr_ref[...] = jax.nn.sigmoid(proj(COL_MRET, D_MODEL)).astype(BF16)
    sn_ref[...] = jax.nn.sigmoid(proj(COL_MRNN, D_MODEL)).astype(BF16)


def _inproj(x2d, g, w_in, cos, sin, seq):
    m = x2d.shape[0]
    tm = INPROJ_ROWS
    seq_tiles = seq // tm

    def rows(width):
        return pl.BlockSpec((tm, width), lambda i: (i, 0))

    out_widths = (RET_QK_WIDTH, RET_QK_WIDTH, RET_V_WIDTH, RET_V_WIDTH,
                  LRU_WIDTH, LRU_WIDTH, D_MODEL, D_MODEL)
    return pl.pallas_call(
        _inproj_kernel,
        grid=(m // tm,),
        in_specs=[
            rows(D_MODEL),
            _const_spec((1, D_MODEL)),
            _const_spec((D_MODEL, IN_WIDTH)),
            pl.BlockSpec((tm, LANES), lambda i: (i % seq_tiles, 0)),
            pl.BlockSpec((tm, LANES), lambda i: (i % seq_tiles, 0)),
        ],
        out_specs=[rows(w) for w in out_widths],
        out_shape=[jax.ShapeDtypeStruct((m, w), BF16) for w in out_widths],
        compiler_params=pltpu.CompilerParams(
            dimension_semantics=("arbitrary",), vmem_limit_bytes=V7X_VMEM_LIMIT),
        name="inproj",
    )(x2d, g, w_in, cos, sin)


def _mixer_kernel(q_ref, k_ref, v_ref, sg_ref, xr_ref, gg_ref, sr_ref, sn_ref, x_ref,
                  dch_ref, din_ref, dq_ref, dk_ref, gn_ref, wro_ref,
                  cw_ref, cb_ref, wa_ref, ba_ref, wi_ref, bi_ref, ap_ref,
                  wrn_ref, wout_ref,
                  o_ref,
                  state_ref, go_ref, xcarry_ref, hcarry_ref, a_sc, u_sc, h_sc):
    rows = MIXER_ROWS

    @pl.when(pl.program_id(1) == 0)
    def _():
        state_ref[...] = jnp.zeros_like(state_ref)
        xcarry_ref[...] = jnp.zeros_like(xcarry_ref)
        hcarry_ref[...] = jnp.zeros_like(hcarry_ref)

    for c in range(rows // RET_CHUNK):
        r0 = c * RET_CHUNK
        for hd in range(RET_HEADS):
            qk = slice(hd * RET_QK_DIM, (hd + 1) * RET_QK_DIM)
            vv = slice(hd * RET_V_DIM, (hd + 1) * RET_V_DIM)
            qh = q_ref[r0:r0 + RET_CHUNK, qk]
            kh = k_ref[r0:r0 + RET_CHUNK, qk]
            vh = v_ref[r0:r0 + RET_CHUNK, vv]
            scores = lax.dot_general(qh, kh, (((1,), (1,)), ((), ())),
                                     preferred_element_type=F32)
            scores = (scores * din_ref[hd]).astype(BF16)
            st = state_ref[hd]
            o = (jnp.dot(scores, vh, preferred_element_type=F32)
                 + jnp.dot(qh, st.astype(BF16), preferred_element_type=F32) * dq_ref[hd])
            kd = (kh.astype(F32) * dk_ref[hd]).astype(BF16)
            state_ref[hd] = st * dch_ref[hd] + lax.dot_general(
                kd, vh, (((0,), (0,)), ((), ())), preferred_element_type=F32)
            mu = jnp.mean(o, axis=-1, keepdims=True)
            dev = o - mu
            var = jnp.mean(dev * dev, axis=-1, keepdims=True)
            y = dev * lax.rsqrt(var + GN_EPS) * gn_ref[:, vv]
            go_ref[r0:r0 + RET_CHUNK, vv] = (
                y * sg_ref[r0:r0 + RET_CHUNK, vv].astype(F32)).astype(BF16)
    y_ret = jnp.dot(go_ref[...], wro_ref[...], preferred_element_type=F32)

    xr = xr_ref[...].astype(F32)
    cat = jnp.concatenate([xcarry_ref[...], xr], axis=0)
    xc = cb_ref[...] + cw_ref[LRU_CONV - 1:LRU_CONV, :] * xr
    for j in range(1, LRU_CONV):
        xc = xc + cw_ref[LRU_CONV - 1 - j:LRU_CONV - j, :] * _shift_rows(cat, j, rows)
    xcarry_ref[...] = xr[rows - SUBLANES:, :]
    xcb = xc.astype(BF16)
    neg_c_softplus = -LRU_C * jax.nn.softplus(-ap_ref[...])
    for n in range(LRU_BLOCKS):
        cols = slice(n * LRU_BLOCK, (n + 1) * LRU_BLOCK)
        xs = xcb[:, cols]
        r = jax.nn.sigmoid(jnp.dot(xs, wa_ref[n], preferred_element_type=F32) + ba_ref[:, cols])
        gi = jax.nn.sigmoid(jnp.dot(xs, wi_ref[n], preferred_element_type=F32) + bi_ref[:, cols])
        log_a = neg_c_softplus[:, cols] * r
        a = jnp.exp(log_a)
        a_sc[:, cols] = a
        u_sc[:, cols] = jnp.sqrt(1.0 - a * a) * gi * xc[:, cols]

    sub = lax.broadcasted_iota(jnp.int32, (SUBLANES, LRU_WIDTH), 0)

    def scan_group(j, h_prev):
        g0 = pl.multiple_of(j * SUBLANES, SUBLANES)
        a = a_sc[pl.ds(g0, SUBLANES), :]
        u = u_sc[pl.ds(g0, SUBLANES), :]
        for d in (1, 2, 4):
            keep = sub >= d
            a_sh = jnp.where(keep, pltpu.roll(a, d, axis=0), 1.0)
            u_sh = jnp.where(keep, pltpu.roll(u, d, axis=0), 0.0)
            u = a * u_sh + u
            a = a * a_sh
        h = u + a * h_prev
        h_sc[pl.ds(g0, SUBLANES), :] = h
        return jnp.broadcast_to(h[SUBLANES - 1:SUBLANES, :], (SUBLANES, LRU_WIDTH))

    hcarry_ref[...] = lax.fori_loop(0, rows // SUBLANES, scan_group, hcarry_ref[...])
    hg = (h_sc[...] * gg_ref[...].astype(F32)).astype(BF16)
    y_rnn = jnp.dot(hg, wrn_ref[...], preferred_element_type=F32)

    mixed = (sr_ref[...].astype(F32) * y_ret + sn_ref[...].astype(F32) * y_rnn).astype(BF16)
    o_ref[...] = x_ref[...] + jnp.dot(mixed, wout_ref[...], preferred_element_type=F32)


def _mixer(proj, x2d, consts, batch, seq):
    q, k, v, sg, xr, gg, sr, sn = proj
    m = x2d.shape[0]
    t = MIXER_ROWS
    tiles = seq // t

    def rows(width):
        return pl.BlockSpec((t, width), lambda b, s: (b * tiles + s, 0))

    const_specs = [pl.BlockSpec(memory_space=pltpu.SMEM)]
    const_specs += [_const_spec(c.shape) for c in consts[1:]]
    return pl.pallas_call(
        _mixer_kernel,
        grid=(batch, tiles),
        in_specs=[rows(RET_QK_WIDTH), rows(RET_QK_WIDTH), rows(RET_V_WIDTH), rows(RET_V_WIDTH),
                  rows(LRU_WIDTH), rows(LRU_WIDTH), rows(D_MODEL), rows(D_MODEL), rows(D_MODEL)]
                 + const_specs,
        out_specs=rows(D_MODEL),
        out_shape=jax.ShapeDtypeStruct((m, D_MODEL), F32),
        scratch_shapes=[
            pltpu.VMEM((RET_HEADS, RET_QK_DIM, RET_V_DIM), F32),
            pltpu.VMEM((t, RET_V_WIDTH), BF16),
            pltpu.VMEM((SUBLANES, LRU_WIDTH), F32),
            pltpu.VMEM((SUBLANES, LRU_WIDTH), F32),
            pltpu.VMEM((t, LRU_WIDTH), F32),
            pltpu.VMEM((t, LRU_WIDTH), F32),
            pltpu.VMEM((t, LRU_WIDTH), F32),
        ],
        compiler_params=pltpu.CompilerParams(
            dimension_semantics=("arbitrary", "arbitrary"), vmem_limit_bytes=V7X_VMEM_LIMIT),
        name="mixer",
    )(q, k, v, sg, xr, gg, sr, sn, x2d, *consts)


def _ffn_kernel(x_ref, g_ref, wup_ref, cw_ref, cb_ref, wdn_ref, fg_ref, o_ref, carry_ref):
    rows = FFN_ROWS

    @pl.when(pl.program_id(1) == 0)
    def _():
        carry_ref[...] = jnp.zeros_like(carry_ref)

    x = x_ref[...]
    h = _rms_norm(x, g_ref[...]).astype(BF16)

    def conv_up(col, width):
        cols = slice(col, col + width)
        up = jnp.dot(h, wup_ref[:, cols], preferred_element_type=F32)
        cat = jnp.concatenate([carry_ref[:, cols], up], axis=0)
        y = cb_ref[:, cols] + cw_ref[FFN_CONV - 1:FFN_CONV, cols] * up
        for j in range(1, FFN_CONV):
            y = y + cw_ref[FFN_CONV - 1 - j:FFN_CONV - j, cols] * _shift_rows(cat, j, rows)
        carry_ref[:, cols] = up[rows - SUBLANES:, :]
        return y

    acc = x
    for col, width in FFN_SPLITS:
        gate = conv_up(col, width)
        val = conv_up(FFN_HIDDEN + col, width)
        act = (jax.nn.silu(gate) * val).astype(BF16)
        acc = acc + jnp.dot(act, wdn_ref[col:col + width, :], preferred_element_type=F32)
    o_ref[...] = _rms_norm(acc, fg_ref[...])


def _ffn(x2d, consts, batch, seq):
    m = x2d.shape[0]
    t = FFN_ROWS
    tiles = seq // t
    rows = pl.BlockSpec((t, D_MODEL), lambda b, s: (b * tiles + s, 0))
    return pl.pallas_call(
        _ffn_kernel,
        grid=(batch, tiles),
        in_specs=[rows] + [_const_spec(c.shape) for c in consts],
        out_specs=rows,
        out_shape=jax.ShapeDtypeStruct((m, D_MODEL), F32),
        scratch_shapes=[pltpu.VMEM((SUBLANES, 2 * FFN_HIDDEN), F32)],
        compiler_params=pltpu.CompilerParams(
            dimension_semantics=("arbitrary", "arbitrary"), vmem_limit_bytes=V7X_VMEM_LIMIT),
        name="convffn",
    )(x2d, *consts)


def _retention_decays():
    heads = jnp.arange(RET_HEADS, dtype=F32)
    log_g = jnp.log1p(-jnp.exp2(-5.0 - heads))
    pos = jnp.arange(RET_CHUNK, dtype=F32)
    rel = pos[:, None] - pos[None, :]
    d_in = jnp.where(rel >= 0, jnp.exp(log_g[:, None, None] * jnp.maximum(rel, 0.0)), 0.0)
    d_q = jnp.exp(log_g[:, None] * (pos + 1.0))
    d_k = jnp.exp(log_g[:, None] * (RET_CHUNK - 1.0 - pos))
    d_chunk = jnp.exp(log_g * RET_CHUNK)
    d_q = jnp.broadcast_to(d_q[:, :, None], (RET_HEADS, RET_CHUNK, RET_V_DIM))
    d_k = jnp.broadcast_to(d_k[:, :, None], (RET_HEADS, RET_CHUNK, RET_QK_DIM))
    return d_chunk, d_in, d_q, d_k


def kernel(x, in_norm_g, w_in, ret_gn_g, w_ret_o, lru_conv_w, lru_conv_b, lru_w_a, lru_b_a,
           lru_w_i, lru_b_i, lru_a_param, w_rnn_o, w_out, ffn_norm_g, w_up, ffn_conv_w,
           ffn_conv_b, w_down, final_norm_g):
    batch, seq, d = x.shape
    depth = w_in.shape[0]
    assert d == D_MODEL and seq % INPROJ_ROWS == 0 and seq % MIXER_ROWS == 0
    assert depth == 1, "the final norm is fused into the (single) layer's FFN kernel"

    pos = jnp.arange(seq, dtype=F32)
    inv_freq = ROPE_BASE ** (-jnp.arange(0, RET_QK_DIM, 2, dtype=F32) / RET_QK_DIM)
    ang = pos[:, None] * inv_freq[None, :]
    cos = jnp.cos(ang)
    sin = jnp.sin(ang)
    d_chunk, d_in, d_q, d_k = _retention_decays()

    x2d = x.reshape(batch * seq, d)
    for l in range(depth):
        proj = _inproj(x2d, in_norm_g[l][None, :], w_in[l].astype(BF16), cos, sin, seq)
        mixer_consts = (
            d_chunk, d_in, d_q, d_k,
            ret_gn_g[l][None, :], w_ret_o[l].astype(BF16),
            lru_conv_w[l], lru_conv_b[l][None, :],
            lru_w_a[l].astype(BF16), lru_b_a[l].reshape(1, LRU_WIDTH),
            lru_w_i[l].astype(BF16), lru_b_i[l].reshape(1, LRU_WIDTH),
            lru_a_param[l][None, :],
            w_rnn_o[l].astype(BF16), w_out[l].astype(BF16),
        )
        x2d = _mixer(proj, x2d, mixer_consts, batch, seq)
        ffn_consts = (
            ffn_norm_g[l][None, :], w_up[l].astype(BF16), ffn_conv_w[l],
            ffn_conv_b[l][None, :], w_down[l].astype(BF16),
            final_norm_g[None, :],
        )
        x2d = _ffn(x2d, ffn_consts, batch, seq)
    return x2d.reshape(batch, seq, d)
```

```python
import jax
import jax.numpy as jnp
from jax import lax
from jax.experimental import pallas as pl
from jax.experimental.pallas import tpu as pltpu

F32 = jnp.float32
BF16 = jnp.bfloat16

D_MODEL = 1024
RET_HEADS = 4
RET_QK_DIM = 256
RET_V_DIM = 512
RET_QK_WIDTH = RET_HEADS * RET_QK_DIM
RET_V_WIDTH = RET_HEADS * RET_V_DIM
RET_CHUNK = 128
ROPE_BASE = 10000.0
LRU_WIDTH = 1024
LRU_BLOCK = 256
LRU_BLOCKS = 4
LRU_CONV = 4
LRU_C = 8.0
FFN_HIDDEN = 2816
FFN_CONV = 3
NORM_EPS = 1e-6
GN_EPS = 1e-5

SUBLANES = 8
LANES = 128
V7X_VMEM_LIMIT = 60 * 1024 * 1024

COL_Q = 0
COL_K = COL_Q + RET_QK_WIDTH
COL_V = COL_K + RET_QK_WIDTH
COL_G = COL_V + RET_V_WIDTH
COL_XR = COL_G + RET_V_WIDTH
COL_GR = COL_XR + LRU_WIDTH
COL_MRET = COL_GR + LRU_WIDTH
COL_MRNN = COL_MRET + D_MODEL
IN_WIDTH = COL_MRNN + D_MODEL

INPROJ_ROWS = 512
MIXER_ROWS = 256
FFN_ROWS = 512
FFN_SPLITS = ((0, 768), (768, 768), (1536, 768), (2304, 512))


def _const_spec(shape):
    nd = len(shape)
    return pl.BlockSpec(shape, lambda *_: (0,) * nd, pipeline_mode=pl.Buffered(1))


def _rms_norm(x, g):
    ms = jnp.mean(x * x, axis=-1, keepdims=True)
    return x * lax.rsqrt(ms + NORM_EPS) * g


def _shift_rows(cat, j, rows):
    return pltpu.roll(cat, j, axis=0)[SUBLANES:SUBLANES + rows, :]


def _inproj_kernel(x_ref, g_ref, w_ref, cos_ref, sin_ref,
                   cw_ref, cb_ref, wa_ref, ba_ref, wi_ref, bi_ref, ap_ref,
                   q_ref, k_ref, v_ref, sg_ref, hg_ref, sr_ref, sn_ref,
                   xcarry_ref, hcarry_ref, a_sc, u_sc):
    rows = INPROJ_ROWS

    @pl.when(pl.program_id(1) == 0)
    def _():
        xcarry_ref[...] = jnp.zeros_like(xcarry_ref)
        hcarry_ref[...] = jnp.zeros_like(hcarry_ref)

    h = _rms_norm(x_ref[...], g_ref[...]).astype(BF16)

    def proj(col, width):
        return jnp.dot(h, w_ref[:, col:col + width], preferred_element_type=F32)

    neg_c_softplus = -LRU_C * jax.nn.softplus(-ap_ref[...])
    sub = lax.broadcasted_iota(jnp.int32, (SUBLANES, LRU_BLOCK), 0)

    cos = cos_ref[...]
    sin = sin_ref[...]

    def rotary_store(acc, out_ref, scale):
        half = RET_QK_DIM // 2
        for hd in range(RET_HEADS):
            lo = hd * RET_QK_DIM
            t1 = acc[:, lo:lo + half]
            t2 = acc[:, lo + half:lo + RET_QK_DIM]
            out_ref[:, lo:lo + half] = ((t1 * cos - t2 * sin) * scale).astype(BF16)
            out_ref[:, lo + half:lo + RET_QK_DIM] = ((t1 * sin + t2 * cos) * scale).astype(BF16)

    step = 1024
    xr = proj(COL_XR, LRU_WIDTH)
    q_acc = proj(COL_Q, RET_QK_WIDTH)

    cat = jnp.concatenate([xcarry_ref[...], xr], axis=0)
    xc = cb_ref[...] + cw_ref[LRU_CONV - 1:LRU_CONV, :] * xr
    for j in range(1, LRU_CONV):
        xc = xc + cw_ref[LRU_CONV - 1 - j:LRU_CONV - j, :] * _shift_rows(cat, j, rows)
    xcarry_ref[...] = xr[rows - SUBLANES:, :]
    xcb = xc.astype(BF16)

    pre_r = []
    pre_i = []
    for n in range(LRU_BLOCKS):
        cols = slice(n * LRU_BLOCK, (n + 1) * LRU_BLOCK)
        pre_r.append(jnp.dot(xcb[:, cols], wa_ref[n], preferred_element_type=F32))
        pre_i.append(jnp.dot(xcb[:, cols], wi_ref[n], preferred_element_type=F32))

    rotary_store(q_acc, q_ref, 1.0)
    k_acc = proj(COL_K, RET_QK_WIDTH)

    for n in range(LRU_BLOCKS):
        cols = slice(n * LRU_BLOCK, (n + 1) * LRU_BLOCK)
        r = jax.nn.sigmoid(pre_r[n] + ba_ref[:, cols])
        gi = jax.nn.sigmoid(pre_i[n] + bi_ref[:, cols])
        a = jnp.exp(neg_c_softplus[:, cols] * r)
        a_sc[:, cols] = a
        u_sc[:, cols] = jnp.sqrt(1.0 - a * a) * gi * xc[:, cols]

    gr_acc = proj(COL_GR, LRU_WIDTH)
    rotary_store(k_acc, k_ref, RET_QK_DIM ** -0.5)
    v_ref[:, 0:step] = proj(COL_V, step).astype(BF16)
    gg = jax.nn.gelu(gr_acc)
    v_ref[:, step:2 * step] = proj(COL_V + step, step).astype(BF16)

    def lru_scan(n):
        cols = slice(n * LRU_BLOCK, (n + 1) * LRU_BLOCK)
        h_prev = hcarry_ref[:, cols]
        for j in range(rows // SUBLANES):
            grp = slice(j * SUBLANES, (j + 1) * SUBLANES)
            a = a_sc[grp, cols]
            u = u_sc[grp, cols]
            for d in (1, 2, 4):
                keep = sub >= d
                a_sh = jnp.where(keep, pltpu.roll(a, d, axis=0), 1.0)
                u_sh = jnp.where(keep, pltpu.roll(u, d, axis=0), 0.0)
                u = a * u_sh + u
                a = a * a_sh
            hj = u + a * h_prev
            hg_ref[grp, cols] = (hj * gg[grp, cols]).astype(BF16)
            h_prev = jnp.broadcast_to(hj[SUBLANES - 1:SUBLANES, :], (SUBLANES, LRU_BLOCK))
        hcarry_ref[:, cols] = h_prev

    lru_scan(0)
    sg_ref[:, 0:step] = jax.nn.silu(proj(COL_G, step)).astype(BF16)
    lru_scan(1)
    sg_ref[:, step:2 * step] = jax.nn.silu(proj(COL_G + step, step)).astype(BF16)
    lru_scan(2)
    sr_ref[...] = jax.nn.sigmoid(proj(COL_MRET, D_MODEL)).astype(BF16)
    lru_scan(3)
    sn_ref[...] = jax.nn.sigmoid(proj(COL_MRNN, D_MODEL)).astype(BF16)


def _inproj(x2d, consts, batch, seq):
    m = x2d.shape[0]
    tm = INPROJ_ROWS
    tiles = seq // tm

    def rows(width):
        return pl.BlockSpec((tm, width), lambda b, s: (b * tiles + s, 0))

    g, w_in, cos, sin = consts[:4]
    lru_consts = consts[4:]
    out_widths = (RET_QK_WIDTH, RET_QK_WIDTH, RET_V_WIDTH, RET_V_WIDTH,
                  LRU_WIDTH, D_MODEL, D_MODEL)
    return pl.pallas_call(
        _inproj_kernel,
        grid=(batch, tiles),
        in_specs=[
            rows(D_MODEL),
            _const_spec(g.shape),
            _const_spec(w_in.shape),
            pl.BlockSpec((tm, LANES), lambda b, s: (s, 0)),
            pl.BlockSpec((tm, LANES), lambda b, s: (s, 0)),
        ] + [_const_spec(c.shape) for c in lru_consts],
        out_specs=[rows(w) for w in out_widths],
        out_shape=[jax.ShapeDtypeStruct((m, w), BF16) for w in out_widths],
        scratch_shapes=[
            pltpu.VMEM((SUBLANES, LRU_WIDTH), F32),
            pltpu.VMEM((SUBLANES, LRU_WIDTH), F32),
            pltpu.VMEM((tm, LRU_WIDTH), F32),
            pltpu.VMEM((tm, LRU_WIDTH), F32),
        ],
        compiler_params=pltpu.CompilerParams(
            dimension_semantics=("arbitrary", "arbitrary"), vmem_limit_bytes=V7X_VMEM_LIMIT),
        name="inproj",
    )(x2d, g, w_in, cos, sin, *lru_consts)


def _mixer_kernel(q_ref, k_ref, v_ref, sg_ref, hg_ref, sr_ref, sn_ref, x_ref,
                  dch_ref, din_ref, dq_ref, dk_ref, gn_ref, wro_ref, wrn_ref, wout_ref,
                  o_ref,
                  state_ref, go_ref):
    rows = MIXER_ROWS

    @pl.when(pl.program_id(1) == 0)
    def _():
        state_ref[...] = jnp.zeros_like(state_ref)

    for c in range(rows // RET_CHUNK):
        r0 = c * RET_CHUNK
        for hd in range(RET_HEADS):
            qk = slice(hd * RET_QK_DIM, (hd + 1) * RET_QK_DIM)
            vv = slice(hd * RET_V_DIM, (hd + 1) * RET_V_DIM)
            qh = q_ref[r0:r0 + RET_CHUNK, qk]
            kh = k_ref[r0:r0 + RET_CHUNK, qk]
            vh = v_ref[r0:r0 + RET_CHUNK, vv]
            scores = lax.dot_general(qh, kh, (((1,), (1,)), ((), ())),
                                     preferred_element_type=F32)
            scores = (scores * din_ref[hd]).astype(BF16)
            st = state_ref[hd]
            o = (jnp.dot(scores, vh, preferred_element_type=F32)
                 + jnp.dot(qh, st.astype(BF16), preferred_element_type=F32) * dq_ref[hd])
            kd = (kh.astype(F32) * dk_ref[hd]).astype(BF16)
            state_ref[hd] = st * dch_ref[hd] + lax.dot_general(
                kd, vh, (((0,), (0,)), ((), ())), preferred_element_type=F32)
            mu = jnp.mean(o, axis=-1, keepdims=True)
            dev = o - mu
            var = jnp.mean(dev * dev, axis=-1, keepdims=True)
            y = dev * lax.rsqrt(var + GN_EPS) * gn_ref[:, vv]
            go_ref[r0:r0 + RET_CHUNK, vv] = (
                y * sg_ref[r0:r0 + RET_CHUNK, vv].astype(F32)).astype(BF16)
    y_ret = jnp.dot(go_ref[...], wro_ref[...], preferred_element_type=F32)
    y_rnn = jnp.dot(hg_ref[...], wrn_ref[...], preferred_element_type=F32)

    mixed = (sr_ref[...].astype(F32) * y_ret + sn_ref[...].astype(F32) * y_rnn).astype(BF16)
    o_ref[...] = x_ref[...] + jnp.dot(mixed, wout_ref[...], preferred_element_type=F32)


def _mixer(proj, x2d, consts, batch, seq):
    q, k, v, sg, hg, sr, sn = proj
    m = x2d.shape[0]
    t = MIXER_ROWS
    tiles = seq // t

    def rows(width):
        return pl.BlockSpec((t, width), lambda b, s: (b * tiles + s, 0))

    const_specs = [pl.BlockSpec(memory_space=pltpu.SMEM)]
    const_specs += [_const_spec(c.shape) for c in consts[1:]]
    return pl.pallas_call(
        _mixer_kernel,
        grid=(batch, tiles),
        in_specs=[rows(RET_QK_WIDTH), rows(RET_QK_WIDTH), rows(RET_V_WIDTH), rows(RET_V_WIDTH),
                  rows(LRU_WIDTH), rows(D_MODEL), rows(D_MODEL), rows(D_MODEL)]
                 + const_specs,
        out_specs=rows(D_MODEL),
        out_shape=jax.ShapeDtypeStruct((m, D_MODEL), F32),
        scratch_shapes=[
            pltpu.VMEM((RET_HEADS, RET_QK_DIM, RET_V_DIM), F32),
            pltpu.VMEM((t, RET_V_WIDTH), BF16),
        ],
        compiler_params=pltpu.CompilerParams(
            dimension_semantics=("arbitrary", "arbitrary"), vmem_limit_bytes=V7X_VMEM_LIMIT),
        name="mixer",
    )(q, k, v, sg, hg, sr, sn, x2d, *consts)


def _ffn_kernel(x_ref, g_ref, wup_ref, cw_ref, cb_ref, wdn_ref, fg_ref, o_ref, carry_ref):
    rows = FFN_ROWS

    @pl.when(pl.program_id(1) == 0)
    def _():
        carry_ref[...] = jnp.zeros_like(carry_ref)

    x = x_ref[...]
    h = _rms_norm(x, g_ref[...]).astype(BF16)

    def conv_up(col, width):
        cols = slice(col, col + width)
        up = jnp.dot(h, wup_ref[:, cols], preferred_element_type=F32)
        cat = jnp.concatenate([carry_ref[:, cols], up], axis=0)
        y = cb_ref[:, cols] + cw_ref[FFN_CONV - 1:FFN_CONV, cols] * up
        for j in range(1, FFN_CONV):
            y = y + cw_ref[FFN_CONV - 1 - j:FFN_CONV - j, cols] * _shift_rows(cat, j, rows)
        carry_ref[:, cols] = up[rows - SUBLANES:, :]
        return y

    acc = x
    nxt = (conv_up(FFN_SPLITS[0][0], FFN_SPLITS[0][1]),
           conv_up(FFN_HIDDEN + FFN_SPLITS[0][0], FFN_SPLITS[0][1]))
    for c, (col, width) in enumerate(FFN_SPLITS):
        gate, val = nxt
        if c + 1 < len(FFN_SPLITS):
            ncol, nwidth = FFN_SPLITS[c + 1]
            nxt = (conv_up(ncol, nwidth), conv_up(FFN_HIDDEN + ncol, nwidth))
        act = (jax.nn.silu(gate) * val).astype(BF16)
        acc = acc + jnp.dot(act, wdn_ref[col:col + width, :], preferred_element_type=F32)
    o_ref[...] = _rms_norm(acc, fg_ref[...])


def _ffn(x2d, consts, batch, seq):
    m = x2d.shape[0]
    t = FFN_ROWS
    tiles = seq // t
    rows = pl.BlockSpec((t, D_MODEL), lambda b, s: (b * tiles + s, 0))
    return pl.pallas_call(
        _ffn_kernel,
        grid=(batch, tiles),
        in_specs=[rows] + [_const_spec(c.shape) for c in consts],
        out_specs=rows,
        out_shape=jax.ShapeDtypeStruct((m, D_MODEL), F32),
        scratch_shapes=[pltpu.VMEM((SUBLANES, 2 * FFN_HIDDEN), F32)],
        compiler_params=pltpu.CompilerParams(
            dimension_semantics=("arbitrary", "arbitrary"), vmem_limit_bytes=V7X_VMEM_LIMIT),
        name="convffn",
    )(x2d, *consts)


def _retention_decays():
    heads = jnp.arange(RET_HEADS, dtype=F32)
    log_g = jnp.log1p(-jnp.exp2(-5.0 - heads))
    pos = jnp.arange(RET_CHUNK, dtype=F32)
    rel = pos[:, None] - pos[None, :]
    d_in = jnp.where(rel >= 0, jnp.exp(log_g[:, None, None] * jnp.maximum(rel, 0.0)), 0.0)
    d_q = jnp.exp(log_g[:, None] * (pos + 1.0))
    d_k = jnp.exp(log_g[:, None] * (RET_CHUNK - 1.0 - pos))
    d_chunk = jnp.exp(log_g * RET_CHUNK)
    d_q = jnp.broadcast_to(d_q[:, :, None], (RET_HEADS, RET_CHUNK, RET_V_DIM))
    d_k = jnp.broadcast_to(d_k[:, :, None], (RET_HEADS, RET_CHUNK, RET_QK_DIM))
    return d_chunk, d_in, d_q, d_k


def kernel(x, in_norm_g, w_in, ret_gn_g, w_ret_o, lru_conv_w, lru_conv_b, lru_w_a, lru_b_a,
           lru_w_i, lru_b_i, lru_a_param, w_rnn_o, w_out, ffn_norm_g, w_up, ffn_conv_w,
           ffn_conv_b, w_down, final_norm_g):
    batch, seq, d = x.shape
    depth = w_in.shape[0]
    assert d == D_MODEL and seq % INPROJ_ROWS == 0 and seq % MIXER_ROWS == 0
    assert depth == 1, "the final norm is fused into the (single) layer's FFN kernel"

    pos = jnp.arange(seq, dtype=F32)
    inv_freq = ROPE_BASE ** (-jnp.arange(0, RET_QK_DIM, 2, dtype=F32) / RET_QK_DIM)
    ang = pos[:, None] * inv_freq[None, :]
    cos = jnp.cos(ang)
    sin = jnp.sin(ang)
    d_chunk, d_in, d_q, d_k = _retention_decays()

    x2d = x.reshape(batch * seq, d)
    for l in range(depth):
        inproj_consts = (
            in_norm_g[l][None, :], w_in[l].astype(BF16), cos, sin,
            lru_conv_w[l], lru_conv_b[l][None, :],
            lru_w_a[l].astype(BF16), lru_b_a[l].reshape(1, LRU_WIDTH),
            lru_w_i[l].astype(BF16), lru_b_i[l].reshape(1, LRU_WIDTH),
            lru_a_param[l][None, :],
        )
        proj = _inproj(x2d, inproj_consts, batch, seq)
        mixer_consts = (
            d_chunk, d_in, d_q, d_k,
            ret_gn_g[l][None, :], w_ret_o[l].astype(BF16),
            w_rnn_o[l].astype(BF16), w_out[l].astype(BF16),
        )
        x2d = _mixer(proj, x2d, mixer_consts, batch, seq)
        ffn_consts = (
            ffn_norm_g[l][None, :], w_up[l].astype(BF16), ffn_conv_w[l],
            ffn_conv_b[l][None, :], w_down[l].astype(BF16),
            final_norm_g[None, :],
        )
        x2d = _ffn(x2d, ffn_consts, batch, seq)
    return x2d.reshape(batch, seq, d)
```

```python
import functools

import jax
import jax.numpy as jnp
from jax import lax
from jax.experimental import pallas as pl
from jax.experimental.pallas import tpu as pltpu

F32 = jnp.float32
BF16 = jnp.bfloat16

D_MODEL = 1024
RET_HEADS = 4
RET_QK_DIM = 256
RET_V_DIM = 512
RET_QK_WIDTH = RET_HEADS * RET_QK_DIM
RET_V_WIDTH = RET_HEADS * RET_V_DIM
RET_CHUNK = 256
ROPE_BASE = 10000.0
LRU_WIDTH = 1024
LRU_BLOCK = 256
LRU_BLOCKS = 4
LRU_CONV = 4
LRU_C = 8.0
FFN_HIDDEN = 2816
FFN_CONV = 3
NORM_EPS = 1e-6
GN_EPS = 1e-5

SUBLANES = 8
LANES = 128
V7X_VMEM_LIMIT = 60 * 1024 * 1024

COL_Q = 0
COL_K = COL_Q + RET_QK_WIDTH
COL_V = COL_K + RET_QK_WIDTH
COL_G = COL_V + RET_V_WIDTH
COL_XR = COL_G + RET_V_WIDTH
COL_GR = COL_XR + LRU_WIDTH
COL_MRET = COL_GR + LRU_WIDTH
COL_MRNN = COL_MRET + D_MODEL
IN_WIDTH = COL_MRNN + D_MODEL

INPROJ_ROWS = 512
MM_COLS = 256
TAIL_ROWS = 64
FENCE_LAG = 2
MIXER_ROWS = 512
FFN_ROWS = 512
FFN_SPLITS = ((0, 768), (768, 768), (1536, 768), (2304, 512))


def _const_spec(shape):
    nd = len(shape)
    return pl.BlockSpec(shape, lambda *_: (0,) * nd, pipeline_mode=pl.Buffered(1))


def _rms_norm(x, g):
    ms = jnp.mean(x * x, axis=-1, keepdims=True)
    return x * lax.rsqrt(ms + NORM_EPS) * g


def _shift_rows(cat, j, rows):
    return pltpu.roll(cat, j, axis=0)[SUBLANES:SUBLANES + rows, :]


def _inproj_kernel(x_ref, g_ref, w_ref, cos_ref, sin_ref,
                   cw_ref, cb_ref, wa_ref, ba_ref, wi_ref, bi_ref, ap_ref,
                   q_ref, k_ref, v_ref, sg_ref, hg_ref, sr_ref, sn_ref,
                   xcarry_ref, hcarry_ref, pre_r_sc, pre_i_sc, xc_sc, gg_sc, h_sc,
                   *, tiles_per_seq, num_tiles):
    rows = INPROJ_ROWS
    t = pl.program_id(0)

    @pl.when(t == 0)
    def _():
        pre_r_sc[...] = jnp.zeros_like(pre_r_sc)
        pre_i_sc[...] = jnp.zeros_like(pre_i_sc)
        xc_sc[...] = jnp.zeros_like(xc_sc)
        gg_sc[...] = jnp.zeros_like(gg_sc)

    @pl.when(jnp.minimum(t, num_tiles - 1) % tiles_per_seq == 0)
    def _():
        xcarry_ref[...] = jnp.zeros_like(xcarry_ref)

    @pl.when(jnp.maximum(t - 1, 0) % tiles_per_seq == 0)
    def _():
        hcarry_ref[...] = jnp.zeros_like(hcarry_ref)

    neg_c_softplus = -LRU_C * jax.nn.softplus(-ap_ref[...])
    first_row = lax.broadcasted_iota(jnp.int32, (SUBLANES, LRU_BLOCK), 0) == 0

    h_carry = [None] * LRU_BLOCKS

    def lru_tail(n, rc):
        cols = slice(n * LRU_BLOCK, (n + 1) * LRU_BLOCK)
        rs = slice(rc * TAIL_ROWS, (rc + 1) * TAIL_ROWS)
        r = jax.nn.sigmoid(pre_r_sc[rs, cols] + ba_ref[:, cols])
        gi = jax.nn.sigmoid(pre_i_sc[rs, cols] + bi_ref[:, cols])
        a_all = jnp.exp(neg_c_softplus[:, cols] * r)
        u_all = jnp.sqrt(1.0 - a_all * a_all) * gi * xc_sc[rs, cols]
        h_prev = hcarry_ref[:, cols] if rc == 0 else h_carry[n]
        for j in range(TAIL_ROWS // SUBLANES):
            grp = slice(j * SUBLANES, (j + 1) * SUBLANES)
            a = a_all[grp, :]
            u = u_all[grp, :] + a * h_prev
            a = jnp.where(first_row, 0.0, a)
            for d in (1, 2):
                u = u + a * pltpu.roll(u, d, axis=0)
                a = a * pltpu.roll(a, d, axis=0)
            hj = u + a * pltpu.roll(u, 4, axis=0)
            out_rows = slice(rc * TAIL_ROWS + j * SUBLANES, rc * TAIL_ROWS + (j + 1) * SUBLANES)
            hg_ref[out_rows, cols] = (hj * gg_sc[out_rows, cols]).astype(BF16)
            h_prev = jnp.where(first_row, pltpu.roll(hj, 1, axis=0), 0.0)
        h_carry[n] = h_prev
        if (rc + 1) * TAIL_ROWS == rows:
            hcarry_ref[:, cols] = h_prev

    h_sc[0:rows, :] = _rms_norm(x_ref[...], g_ref[...]).astype(BF16)
    cos = cos_ref[...]
    sin = sin_ref[...]

    def mm(col):
        return jnp.dot(h_sc[0:rows, :], w_ref[:, col:col + MM_COLS], preferred_element_type=F32)

    fence_row = rows + lax.shift_right_arithmetic(t, 31)

    def fence(hp):
        mark = jnp.concatenate([hp[:, :LANES], hp[:, LANES:2 * LANES]], axis=0).astype(BF16)
        h_sc[pl.ds(pl.multiple_of(fence_row, 2 * SUBLANES), 2 * SUBLANES), 0:LANES] = mark

    def rotary(acc, out_ref, lo, scale):
        half = RET_QK_DIM // 2
        t1 = acc[:, :half]
        t2 = acc[:, half:]
        out_ref[:, lo:lo + half] = ((t1 * cos - t2 * sin) * scale).astype(BF16)
        out_ref[:, lo + half:lo + RET_QK_DIM] = ((t1 * sin + t2 * cos) * scale).astype(BF16)

    def section(kind, i):
        lo = i * MM_COLS
        if kind == "q":
            rotary(mm(COL_Q + lo), q_ref, lo, 1.0)
        elif kind == "k":
            rotary(mm(COL_K + lo), k_ref, lo, RET_QK_DIM ** -0.5)
        elif kind == "v":
            v_ref[:, lo:lo + MM_COLS] = mm(COL_V + lo).astype(BF16)
        elif kind == "g":
            sg_ref[:, lo:lo + MM_COLS] = jax.nn.silu(mm(COL_G + lo)).astype(BF16)
        elif kind == "mret":
            sr_ref[:, lo:lo + MM_COLS] = jax.nn.sigmoid(mm(COL_MRET + lo)).astype(BF16)
        else:
            sn_ref[:, lo:lo + MM_COLS] = jax.nn.sigmoid(mm(COL_MRNN + lo)).astype(BF16)

    def lru_front_a(n):
        cols = slice(n * LRU_BLOCK, (n + 1) * LRU_BLOCK)
        xr = mm(COL_XR + n * LRU_BLOCK)
        cat = jnp.concatenate([xcarry_ref[:, cols], xr], axis=0)
        xc = cb_ref[:, cols] + cw_ref[LRU_CONV - 1:LRU_CONV, cols] * xr
        for j in range(1, LRU_CONV):
            xc = xc + cw_ref[LRU_CONV - 1 - j:LRU_CONV - j, cols] * _shift_rows(cat, j, rows)
        xcarry_ref[:, cols] = xr[rows - SUBLANES:, :]
        xc_sc[:, cols] = xc
        return xc.astype(BF16)

    def lru_front_b(n, xcb):
        cols = slice(n * LRU_BLOCK, (n + 1) * LRU_BLOCK)
        gg_sc[:, cols] = jax.nn.gelu(mm(COL_GR + n * LRU_BLOCK))
        pre_r_sc[:, cols] = jnp.dot(xcb, wa_ref[n], preferred_element_type=F32)
        pre_i_sc[:, cols] = jnp.dot(xcb, wi_ref[n], preferred_element_type=F32)

    tail_chunks = rows // TAIL_ROWS
    others = ([("v", i) for i in range(RET_V_WIDTH // MM_COLS)]
              + [("q", i) for i in range(RET_HEADS)] + [("k", i) for i in range(RET_HEADS)]
              + [("g", i) for i in range(RET_V_WIDTH // MM_COLS)]
              + [("mret", i) for i in range(D_MODEL // MM_COLS)]
              + [("mrnn", i) for i in range(D_MODEL // MM_COLS)])
    others.reverse()
    pending = []
    for n in range(LRU_BLOCKS):
        for rc in range(tail_chunks):
            lru_tail(n, rc)
            pending.append(h_carry[n])
            if len(pending) > FENCE_LAG:
                fence(pending.pop(0))
            if n == 0 or rc >= 2:
                section(*others.pop())
        xcb = lru_front_a(n)
        section(*others.pop())
        lru_front_b(n, xcb)
    while others:
        section(*others.pop())


def _inproj(x2d, consts, batch, seq):
    m = x2d.shape[0]
    tm = INPROJ_ROWS
    tiles = seq // tm
    num_tiles = m // tm

    def cur(width):
        return pl.BlockSpec((tm, width), lambda t: (jnp.minimum(t, num_tiles - 1), 0))

    def table():
        return pl.BlockSpec((tm, LANES), lambda t: (jnp.minimum(t, num_tiles - 1) % tiles, 0))

    prev = pl.BlockSpec((tm, LRU_WIDTH), lambda t: (jnp.maximum(t - 1, 0), 0))

    g, w_in, cos, sin = consts[:4]
    lru_consts = consts[4:]
    out_widths = (RET_QK_WIDTH, RET_QK_WIDTH, RET_V_WIDTH, RET_V_WIDTH,
                  LRU_WIDTH, D_MODEL, D_MODEL)
    out_specs = [cur(w) for w in out_widths]
    out_specs[4] = prev
    staged = pltpu.VMEM((tm, LRU_WIDTH), F32)
    return pl.pallas_call(
        functools.partial(_inproj_kernel, tiles_per_seq=tiles, num_tiles=num_tiles),
        grid=(num_tiles + 1,),
        in_specs=[
            cur(D_MODEL),
            _const_spec(g.shape),
            _const_spec(w_in.shape),
            table(),
            table(),
        ] + [_const_spec(c.shape) for c in lru_consts],
        out_specs=out_specs,
        out_shape=[jax.ShapeDtypeStruct((m, w), BF16) for w in out_widths],
        scratch_shapes=[
            pltpu.VMEM((SUBLANES, LRU_WIDTH), F32),
            pltpu.VMEM((SUBLANES, LRU_WIDTH), F32),
            staged, staged, staged, staged,
            pltpu.VMEM((tm + 2 * SUBLANES, D_MODEL), BF16),
        ],
        compiler_params=pltpu.CompilerParams(
            dimension_semantics=("arbitrary",), vmem_limit_bytes=V7X_VMEM_LIMIT),
        name="inproj",
    )(x2d, g, w_in, cos, sin, *lru_consts)


def _mixer_kernel(q_ref, k_ref, v_ref, sg_ref, hg_ref, sr_ref, sn_ref, x_ref,
                  dch_ref, din_ref, dq_ref, dk_ref, gn_ref, wro_ref, wrn_ref, wout_ref,
                  o_ref,
                  state_ref, go_ref):
    rows = MIXER_ROWS

    @pl.when(pl.program_id(1) == 0)
    def _():
        state_ref[...] = jnp.zeros_like(state_ref)

    for c in range(rows // RET_CHUNK):
        r0 = c * RET_CHUNK
        for hd in range(RET_HEADS):
            qk = slice(hd * RET_QK_DIM, (hd + 1) * RET_QK_DIM)
            vv = slice(hd * RET_V_DIM, (hd + 1) * RET_V_DIM)
            qh = q_ref[r0:r0 + RET_CHUNK, qk]
            kh = k_ref[r0:r0 + RET_CHUNK, qk]
            vh = v_ref[r0:r0 + RET_CHUNK, vv]
            scores = lax.dot_general(qh, kh, (((1,), (1,)), ((), ())),
                                     preferred_element_type=F32)
            scores = (scores * din_ref[hd]).astype(BF16)
            st = state_ref[hd]
            o = (jnp.dot(scores, vh, preferred_element_type=F32)
                 + jnp.dot(qh, st.astype(BF16), preferred_element_type=F32) * dq_ref[hd])
            kd = (kh.astype(F32) * dk_ref[hd]).astype(BF16)
            state_ref[hd] = st * dch_ref[hd] + lax.dot_general(
                kd, vh, (((0,), (0,)), ((), ())), preferred_element_type=F32)
            mu = jnp.mean(o, axis=-1, keepdims=True)
            dev = o - mu
            var = jnp.mean(dev * dev, axis=-1, keepdims=True)
            y = dev * lax.rsqrt(var + GN_EPS) * gn_ref[:, vv]
            go_ref[r0:r0 + RET_CHUNK, vv] = (
                y * sg_ref[r0:r0 + RET_CHUNK, vv].astype(F32)).astype(BF16)
    y_ret = jnp.dot(go_ref[...], wro_ref[...], preferred_element_type=F32)
    y_rnn = jnp.dot(hg_ref[...], wrn_ref[...], preferred_element_type=F32)

    mixed = (sr_ref[...].astype(F32) * y_ret + sn_ref[...].astype(F32) * y_rnn).astype(BF16)
    o_ref[...] = x_ref[...] + jnp.dot(mixed, wout_ref[...], preferred_element_type=F32)


def _mixer(proj, x2d, consts, batch, seq):
    q, k, v, sg, hg, sr, sn = proj
    m = x2d.shape[0]
    t = MIXER_ROWS
    tiles = seq // t

    def rows(width):
        return pl.BlockSpec((t, width), lambda b, s: (b * tiles + s, 0))

    const_specs = [pl.BlockSpec(memory_space=pltpu.SMEM)]
    const_specs += [_const_spec(c.shape) for c in consts[1:]]
    return pl.pallas_call(
        _mixer_kernel,
        grid=(batch, tiles),
        in_specs=[rows(RET_QK_WIDTH), rows(RET_QK_WIDTH), rows(RET_V_WIDTH), rows(RET_V_WIDTH),
                  rows(LRU_WIDTH), rows(D_MODEL), rows(D_MODEL), rows(D_MODEL)]
                 + const_specs,
        out_specs=rows(D_MODEL),
        out_shape=jax.ShapeDtypeStruct((m, D_MODEL), F32),
        scratch_shapes=[
            pltpu.VMEM((RET_HEADS, RET_QK_DIM, RET_V_DIM), F32),
            pltpu.VMEM((t, RET_V_WIDTH), BF16),
        ],
        compiler_params=pltpu.CompilerParams(
            dimension_semantics=("arbitrary", "arbitrary"), vmem_limit_bytes=V7X_VMEM_LIMIT),
        name="mixer",
    )(q, k, v, sg, hg, sr, sn, x2d, *consts)


def _ffn_kernel(x_ref, g_ref, wup_ref, cw_ref, cb_ref, wdn_ref, fg_ref, o_ref, carry_ref):
    rows = FFN_ROWS

    @pl.when(pl.program_id(1) == 0)
    def _():
        carry_ref[...] = jnp.zeros_like(carry_ref)

    x = x_ref[...]
    h = _rms_norm(x, g_ref[...]).astype(BF16)

    def conv_up(col, width):
        cols = slice(col, col + width)
        up = jnp.dot(h, wup_ref[:, cols], preferred_element_type=F32)
        cat = jnp.concatenate([carry_ref[:, cols], up], axis=0)
        y = cb_ref[:, cols] + cw_ref[FFN_CONV - 1:FFN_CONV, cols] * up
        for j in range(1, FFN_CONV):
            y = y + cw_ref[FFN_CONV - 1 - j:FFN_CONV - j, cols] * _shift_rows(cat, j, rows)
        carry_ref[:, cols] = up[rows - SUBLANES:, :]
        return y

    acc = x
    nxt = (conv_up(FFN_SPLITS[0][0], FFN_SPLITS[0][1]),
           conv_up(FFN_HIDDEN + FFN_SPLITS[0][0], FFN_SPLITS[0][1]))
    for c, (col, width) in enumerate(FFN_SPLITS):
        gate, val = nxt
        if c + 1 < len(FFN_SPLITS):
            ncol, nwidth = FFN_SPLITS[c + 1]
            nxt = (conv_up(ncol, nwidth), conv_up(FFN_HIDDEN + ncol, nwidth))
        act = (jax.nn.silu(gate) * val).astype(BF16)
        acc = acc + jnp.dot(act, wdn_ref[col:col + width, :], preferred_element_type=F32)
    o_ref[...] = _rms_norm(acc, fg_ref[...])


def _ffn(x2d, consts, batch, seq):
    m = x2d.shape[0]
    t = FFN_ROWS
    tiles = seq // t
    rows = pl.BlockSpec((t, D_MODEL), lambda b, s: (b * tiles + s, 0))
    return pl.pallas_call(
        _ffn_kernel,
        grid=(batch, tiles),
        in_specs=[rows] + [_const_spec(c.shape) for c in consts],
        out_specs=rows,
        out_shape=jax.ShapeDtypeStruct((m, D_MODEL), F32),
        scratch_shapes=[pltpu.VMEM((SUBLANES, 2 * FFN_HIDDEN), F32)],
        compiler_params=pltpu.CompilerParams(
            dimension_semantics=("arbitrary", "arbitrary"), vmem_limit_bytes=V7X_VMEM_LIMIT),
        name="convffn",
    )(x2d, *consts)


def _retention_decays():
    heads = jnp.arange(RET_HEADS, dtype=F32)
    log_g = jnp.log1p(-jnp.exp2(-5.0 - heads))
    pos = jnp.arange(RET_CHUNK, dtype=F32)
    rel = pos[:, None] - pos[None, :]
    d_in = jnp.where(rel >= 0, jnp.exp(log_g[:, None, None] * jnp.maximum(rel, 0.0)), 0.0)
    d_q = jnp.exp(log_g[:, None] * (pos + 1.0))
    d_k = jnp.exp(log_g[:, None] * (RET_CHUNK - 1.0 - pos))
    d_chunk = jnp.exp(log_g * RET_CHUNK)
    d_q = jnp.broadcast_to(d_q[:, :, None], (RET_HEADS, RET_CHUNK, RET_V_DIM))
    d_k = jnp.broadcast_to(d_k[:, :, None], (RET_HEADS, RET_CHUNK, RET_QK_DIM))
    return d_chunk, d_in, d_q, d_k


def kernel(x, in_norm_g, w_in, ret_gn_g, w_ret_o, lru_conv_w, lru_conv_b, lru_w_a, lru_b_a,
           lru_w_i, lru_b_i, lru_a_param, w_rnn_o, w_out, ffn_norm_g, w_up, ffn_conv_w,
           ffn_conv_b, w_down, final_norm_g):
    batch, seq, d = x.shape
    depth = w_in.shape[0]
    assert d == D_MODEL and seq % INPROJ_ROWS == 0 and seq % MIXER_ROWS == 0
    assert depth == 1, "the final norm is fused into the (single) layer's FFN kernel"

    pos = jnp.arange(seq, dtype=F32)
    inv_freq = ROPE_BASE ** (-jnp.arange(0, RET_QK_DIM, 2, dtype=F32) / RET_QK_DIM)
    ang = pos[:, None] * inv_freq[None, :]
    cos = jnp.cos(ang)
    sin = jnp.sin(ang)
    d_chunk, d_in, d_q, d_k = _retention_decays()

    x2d = x.reshape(batch * seq, d)
    for l in range(depth):
        inproj_consts = (
            in_norm_g[l][None, :], w_in[l].astype(BF16), cos, sin,
            lru_conv_w[l], lru_conv_b[l][None, :],
            lru_w_a[l].astype(BF16), lru_b_a[l].reshape(1, LRU_WIDTH),
            lru_w_i[l].astype(BF16), lru_b_i[l].reshape(1, LRU_WIDTH),
            lru_a_param[l][None, :],
        )
        proj = _inproj(x2d, inproj_consts, batch, seq)
        mixer_consts = (
            d_chunk, d_in, d_q, d_k,
            ret_gn_g[l][None, :], w_ret_o[l].astype(BF16),
            w_rnn_o[l].astype(BF16), w_out[l].astype(BF16),
        )
        x2d = _mixer(proj, x2d, mixer_consts, batch, seq)
        ffn_consts = (
            ffn_norm_g[l][None, :], w_up[l].astype(BF16), ffn_conv_w[l],
            ffn_conv_b[l][None, :], w_down[l].astype(BF16),
            final_norm_g[None, :],
        )
        x2d = _ffn(x2d, ffn_consts, batch, seq)
    return x2d.reshape(batch, seq, d)
```

```python
import functools
import math

import jax
import jax.numpy as jnp
from jax import lax
from jax.experimental import pallas as pl
from jax.experimental.pallas import tpu as pltpu

F32 = jnp.float32
BF16 = jnp.bfloat16

D_MODEL = 1024
RET_HEADS = 4
RET_QK_DIM = 256
RET_V_DIM = 512
RET_QK_WIDTH = RET_HEADS * RET_QK_DIM
RET_V_WIDTH = RET_HEADS * RET_V_DIM
RET_CHUNK = 256
ROPE_BASE = 10000.0
LRU_WIDTH = 1024
LRU_BLOCK = 256
LRU_BLOCKS = 4
LRU_CONV = 4
LRU_C = 8.0
FFN_HIDDEN = 2816
FFN_CONV = 3
NORM_EPS = 1e-6
GN_EPS = 1e-5
GELU_A = 2.0 * (2.0 / math.pi) ** 0.5
GELU_B = 0.044715 * GELU_A
TINY = 1e-30

SUBLANES = 8
LANES = 128
V7X_VMEM_LIMIT = 60 * 1024 * 1024

COL_Q = 0
COL_K = COL_Q + RET_QK_WIDTH
COL_V = COL_K + RET_QK_WIDTH
COL_G = COL_V + RET_V_WIDTH
COL_XR = COL_G + RET_V_WIDTH
COL_GR = COL_XR + LRU_WIDTH
COL_MRET = COL_GR + LRU_WIDTH
COL_MRNN = COL_MRET + D_MODEL
IN_WIDTH = COL_MRNN + D_MODEL

INPROJ_ROWS = 512
MM_COLS = 256
TAIL_ROWS = 32
SLOTS_PER_BLOCK = 8
FENCE_LAG = 2
FENCE_PAD = 2
FRONT_A_CHUNK = 3
MIXER_ROWS = 512
FFN_ROWS = 512
FFN_SPLITS = ((0, 768), (768, 768), (1536, 768), (2304, 512))


def _const_spec(shape):
    nd = len(shape)
    return pl.BlockSpec(shape, lambda *_: (0,) * nd, pipeline_mode=pl.Buffered(1))


def _rms_norm(x, g):
    ms = jnp.mean(x * x, axis=-1, keepdims=True)
    return x * lax.rsqrt(ms + NORM_EPS) * g


def _gelu_tanh(x):
    two_z = x * (GELU_A + GELU_B * (x * x))
    return x / (1.0 + jnp.exp(-two_z))


def _shift_rows(cat, j, rows):
    return pltpu.roll(cat, j, axis=0)[SUBLANES:SUBLANES + rows, :]


def _inproj_kernel(x_ref, g_ref, w_ref, cos_ref, sin_ref,
                   cw_ref, cb_ref, wa_ref, ba_ref, wi_ref, bi_ref, ap_ref,
                   q_ref, k_ref, v_ref, sg_ref, hg_ref, sr_ref, sn_ref,
                   xcarry_ref, hcarry_ref, pre_r_sc, pre_i_sc, xc_sc, gg_sc, h_sc,
                   *, tiles_per_seq, num_tiles):
    rows = INPROJ_ROWS
    t = pl.program_id(0)

    @pl.when(t == 0)
    def _():
        pre_r_sc[...] = jnp.zeros_like(pre_r_sc)
        pre_i_sc[...] = jnp.zeros_like(pre_i_sc)
        xc_sc[...] = jnp.zeros_like(xc_sc)
        gg_sc[...] = jnp.zeros_like(gg_sc)

    @pl.when(jnp.minimum(t, num_tiles - 1) % tiles_per_seq == 0)
    def _():
        xcarry_ref[...] = jnp.zeros_like(xcarry_ref)

    @pl.when(jnp.maximum(t - 1, 0) % tiles_per_seq == 0)
    def _():
        hcarry_ref[...] = jnp.zeros_like(hcarry_ref)

    neg_c_softplus = -LRU_C * jax.nn.softplus(-ap_ref[...])
    first_row = lax.broadcasted_iota(jnp.int32, (SUBLANES, LRU_BLOCK), 0) == 0

    h_carry = [None] * LRU_BLOCKS

    def lru_tail(n, rc):
        cols = slice(n * LRU_BLOCK, (n + 1) * LRU_BLOCK)
        rs = slice(rc * TAIL_ROWS, (rc + 1) * TAIL_ROWS)
        r = jax.nn.sigmoid(pre_r_sc[rs, cols] + ba_ref[:, cols])
        gi = jax.nn.sigmoid(pre_i_sc[rs, cols] + bi_ref[:, cols])
        a_all = jnp.exp(neg_c_softplus[:, cols] * r)
        y = 1.0 - a_all * a_all
        u_all = y * lax.rsqrt(jnp.maximum(y, TINY)) * gi * xc_sc[rs, cols]
        h_prev = hcarry_ref[:, cols] if rc == 0 else h_carry[n]
        for j in range(TAIL_ROWS // SUBLANES):
            grp = slice(j * SUBLANES, (j + 1) * SUBLANES)
            a = a_all[grp, :]
            u = u_all[grp, :] + a * h_prev
            a = jnp.where(first_row, 0.0, a)
            for d in (1, 2):
                u = u + a * pltpu.roll(u, d, axis=0)
                a = a * pltpu.roll(a, d, axis=0)
            hj = u + a * pltpu.roll(u, 4, axis=0)
            out_rows = slice(rc * TAIL_ROWS + j * SUBLANES, rc * TAIL_ROWS + (j + 1) * SUBLANES)
            hg_ref[out_rows, cols] = (hj * gg_sc[out_rows, cols]).astype(BF16)
            h_prev = jnp.where(first_row, pltpu.roll(hj, 1, axis=0), 0.0)
        h_carry[n] = h_prev
        if (rc + 1) * TAIL_ROWS == rows:
            hcarry_ref[:, cols] = h_prev

    h_sc[0:rows, :] = _rms_norm(x_ref[...], g_ref[...]).astype(BF16)
    cos = cos_ref[...]
    sin = sin_ref[...]

    fence_row = rows + lax.shift_right_arithmetic(t, 31)

    def fence(hp, pad):
        mark = jnp.concatenate([hp[:, :LANES], hp[:, LANES:2 * LANES]], axis=0)
        for _ in range(pad):
            mark = pltpu.roll(mark, 1, axis=1)
        h_sc[pl.ds(pl.multiple_of(fence_row, 2 * SUBLANES), 2 * SUBLANES), 0:LANES] = mark.astype(BF16)

    def mm(col):
        return jnp.dot(h_sc[0:rows, :], w_ref[:, col:col + MM_COLS], preferred_element_type=F32)

    cos_k = cos * RET_QK_DIM ** -0.5
    sin_k = sin * RET_QK_DIM ** -0.5

    def rotary(acc, out_ref, lo, c, s):
        half = RET_QK_DIM // 2
        t1 = acc[:, :half]
        t2 = acc[:, half:]
        out_ref[:, lo:lo + half] = (t1 * c - t2 * s).astype(BF16)
        out_ref[:, lo + half:lo + RET_QK_DIM] = (t1 * s + t2 * c).astype(BF16)

    def section(kind, i):
        lo = i * MM_COLS
        if kind == "q":
            rotary(mm(COL_Q + lo), q_ref, lo, cos, sin)
        elif kind == "k":
            rotary(mm(COL_K + lo), k_ref, lo, cos_k, sin_k)
        elif kind == "v":
            v_ref[:, lo:lo + MM_COLS] = mm(COL_V + lo).astype(BF16)
        elif kind == "g":
            sg_ref[:, lo:lo + MM_COLS] = jax.nn.silu(mm(COL_G + lo)).astype(BF16)
        elif kind == "mret":
            sr_ref[:, lo:lo + MM_COLS] = jax.nn.sigmoid(mm(COL_MRET + lo)).astype(BF16)
        else:
            sn_ref[:, lo:lo + MM_COLS] = jax.nn.sigmoid(mm(COL_MRNN + lo)).astype(BF16)

    def lru_front_a(n):
        cols = slice(n * LRU_BLOCK, (n + 1) * LRU_BLOCK)
        xr = mm(COL_XR + n * LRU_BLOCK)
        cat = jnp.concatenate([xcarry_ref[:, cols], xr], axis=0)
        xc = cb_ref[:, cols] + cw_ref[LRU_CONV - 1:LRU_CONV, cols] * xr
        for j in range(1, LRU_CONV):
            xc = xc + cw_ref[LRU_CONV - 1 - j:LRU_CONV - j, cols] * _shift_rows(cat, j, rows)
        xcarry_ref[:, cols] = xr[rows - SUBLANES:, :]
        return xc

    def lru_front_b(n, xc):
        cols = slice(n * LRU_BLOCK, (n + 1) * LRU_BLOCK)
        xcb = xc.astype(BF16)
        gg_sc[:, cols] = _gelu_tanh(mm(COL_GR + n * LRU_BLOCK))
        pre_r_sc[:, cols] = jnp.dot(xcb, wa_ref[n], preferred_element_type=F32)
        pre_i_sc[:, cols] = jnp.dot(xcb, wi_ref[n], preferred_element_type=F32)
        xc_sc[:, cols] = xc

    tail_chunks = rows // TAIL_ROWS
    heavy = ([("q", i) for i in range(RET_HEADS)] + [("k", i) for i in range(RET_HEADS)]
             + [("g", i) for i in range(RET_V_WIDTH // MM_COLS)]
             + [("mret", i) for i in range(D_MODEL // MM_COLS)]
             + [("mrnn", i) for i in range(D_MODEL // MM_COLS)])
    light = [("v", i) for i in range(RET_V_WIDTH // MM_COLS)]
    others = []
    every = len(heavy) // len(light)
    for i, item in enumerate(heavy):
        others.append(item)
        if i % every == 1:
            others.append(light.pop(0))
    assert not light
    others.reverse()
    pending = []
    per_slot = tail_chunks // SLOTS_PER_BLOCK
    for n in range(LRU_BLOCKS):
        for rc in range(tail_chunks):
            lru_tail(n, rc)
            pending.append(h_carry[n])
            if len(pending) > FENCE_LAG:
                fence(pending.pop(0), FENCE_PAD)
            if rc % per_slot != per_slot - 1:
                continue
            slot = rc // per_slot
            if n == 0 or slot >= 1:
                section(*others.pop())
            if slot == FRONT_A_CHUNK:
                xc_new = lru_front_a(n)
        lru_front_b(n, xc_new)
    while others:
        section(*others.pop())


def _inproj(x2d, consts, batch, seq):
    m = x2d.shape[0]
    tm = INPROJ_ROWS
    tiles = seq // tm
    num_tiles = m // tm

    def cur(width):
        return pl.BlockSpec((tm, width), lambda t: (jnp.minimum(t, num_tiles - 1), 0))

    def table():
        return pl.BlockSpec((tm, LANES), lambda t: (jnp.minimum(t, num_tiles - 1) % tiles, 0))

    prev = pl.BlockSpec((tm, LRU_WIDTH), lambda t: (jnp.maximum(t - 1, 0), 0))

    g, w_in, cos, sin = consts[:4]
    lru_consts = consts[4:]
    out_widths = (RET_QK_WIDTH, RET_QK_WIDTH, RET_V_WIDTH, RET_V_WIDTH,
                  LRU_WIDTH, D_MODEL, D_MODEL)
    out_specs = [cur(w) for w in out_widths]
    out_specs[4] = prev
    staged = pltpu.VMEM((tm, LRU_WIDTH), F32)
    return pl.pallas_call(
        functools.partial(_inproj_kernel, tiles_per_seq=tiles, num_tiles=num_tiles),
        grid=(num_tiles + 1,),
        in_specs=[
            cur(D_MODEL),
            _const_spec(g.shape),
            _const_spec(w_in.shape),
            table(),
            table(),
        ] + [_const_spec(c.shape) for c in lru_consts],
        out_specs=out_specs,
        out_shape=[jax.ShapeDtypeStruct((m, w), BF16) for w in out_widths],
        scratch_shapes=[
            pltpu.VMEM((SUBLANES, LRU_WIDTH), F32),
            pltpu.VMEM((SUBLANES, LRU_WIDTH), F32),
            staged, staged, staged, staged,
            pltpu.VMEM((tm + 2 * SUBLANES, D_MODEL), BF16),
        ],
        compiler_params=pltpu.CompilerParams(
            dimension_semantics=("arbitrary",), vmem_limit_bytes=V7X_VMEM_LIMIT),
        name="inproj",
    )(x2d, g, w_in, cos, sin, *lru_consts)


def _mixer_kernel(q_ref, k_ref, v_ref, sg_ref, hg_ref, sr_ref, sn_ref, x_ref,
                  dch_ref, din_ref, dq_ref, dk_ref, gn_ref, wro_ref, wrn_ref, wout_ref,
                  o_ref,
                  state_ref, go_ref):
    rows = MIXER_ROWS

    @pl.when(pl.program_id(1) == 0)
    def _():
        state_ref[...] = jnp.zeros_like(state_ref)

    for c in range(rows // RET_CHUNK):
        r0 = c * RET_CHUNK
        for hd in range(RET_HEADS):
            qk = slice(hd * RET_QK_DIM, (hd + 1) * RET_QK_DIM)
            vv = slice(hd * RET_V_DIM, (hd + 1) * RET_V_DIM)
            qh = q_ref[r0:r0 + RET_CHUNK, qk]
            kh = k_ref[r0:r0 + RET_CHUNK, qk]
            vh = v_ref[r0:r0 + RET_CHUNK, vv]
            scores = lax.dot_general(qh, kh, (((1,), (1,)), ((), ())),
                                     preferred_element_type=F32)
            scores = (scores * din_ref[hd]).astype(BF16)
            st = state_ref[hd]
            o = (jnp.dot(scores, vh, preferred_element_type=F32)
                 + jnp.dot(qh, st.astype(BF16), preferred_element_type=F32) * dq_ref[hd])
            kd = (kh.astype(F32) * dk_ref[hd]).astype(BF16)
            state_ref[hd] = st * dch_ref[hd] + lax.dot_general(
                kd, vh, (((0,), (0,)), ((), ())), preferred_element_type=F32)
            mu = jnp.mean(o, axis=-1, keepdims=True)
            dev = o - mu
            var = jnp.mean(dev * dev, axis=-1, keepdims=True)
            y = dev * lax.rsqrt(var + GN_EPS) * gn_ref[:, vv]
            go_ref[r0:r0 + RET_CHUNK, vv] = (
                y * sg_ref[r0:r0 + RET_CHUNK, vv].astype(F32)).astype(BF16)
    y_ret = jnp.dot(go_ref[...], wro_ref[...], preferred_element_type=F32)
    y_rnn = jnp.dot(hg_ref[...], wrn_ref[...], preferred_element_type=F32)

    mixed = (sr_ref[...].astype(F32) * y_ret + sn_ref[...].astype(F32) * y_rnn).astype(BF16)
    o_ref[...] = x_ref[...] + jnp.dot(mixed, wout_ref[...], preferred_element_type=F32)


def _mixer(proj, x2d, consts, batch, seq):
    q, k, v, sg, hg, sr, sn = proj
    m = x2d.shape[0]
    t = MIXER_ROWS
    tiles = seq // t

    def rows(width):
        return pl.BlockSpec((t, width), lambda b, s: (b * tiles + s, 0))

    const_specs = [pl.BlockSpec(memory_space=pltpu.SMEM)]
    const_specs += [_const_spec(c.shape) for c in consts[1:]]
    return pl.pallas_call(
        _mixer_kernel,
        grid=(batch, tiles),
        in_specs=[rows(RET_QK_WIDTH), rows(RET_QK_WIDTH), rows(RET_V_WIDTH), rows(RET_V_WIDTH),
                  rows(LRU_WIDTH), rows(D_MODEL), rows(D_MODEL), rows(D_MODEL)]
                 + const_specs,
        out_specs=rows(D_MODEL),
        out_shape=jax.ShapeDtypeStruct((m, D_MODEL), F32),
        scratch_shapes=[
            pltpu.VMEM((RET_HEADS, RET_QK_DIM, RET_V_DIM), F32),
            pltpu.VMEM((t, RET_V_WIDTH), BF16),
        ],
        compiler_params=pltpu.CompilerParams(
            dimension_semantics=("arbitrary", "arbitrary"), vmem_limit_bytes=V7X_VMEM_LIMIT),
        name="mixer",
    )(q, k, v, sg, hg, sr, sn, x2d, *consts)


def _ffn_kernel(x_ref, g_ref, wup_ref, cw_ref, cb_ref, wdn_ref, fg_ref, o_ref, carry_ref):
    rows = FFN_ROWS

    @pl.when(pl.program_id(1) == 0)
    def _():
        carry_ref[...] = jnp.zeros_like(carry_ref)

    x = x_ref[...]
    h = _rms_norm(x, g_ref[...]).astype(BF16)

    def conv_up(col, width):
        cols = slice(col, col + width)
        up = jnp.dot(h, wup_ref[:, cols], preferred_element_type=F32)
        cat = jnp.concatenate([carry_ref[:, cols], up], axis=0)
        y = cb_ref[:, cols] + cw_ref[FFN_CONV - 1:FFN_CONV, cols] * up
        for j in range(1, FFN_CONV):
            y = y + cw_ref[FFN_CONV - 1 - j:FFN_CONV - j, cols] * _shift_rows(cat, j, rows)
        carry_ref[:, cols] = up[rows - SUBLANES:, :]
        return y

    acc = x
    nxt = (conv_up(FFN_SPLITS[0][0], FFN_SPLITS[0][1]),
           conv_up(FFN_HIDDEN + FFN_SPLITS[0][0], FFN_SPLITS[0][1]))
    for c, (col, width) in enumerate(FFN_SPLITS):
        gate, val = nxt
        if c + 1 < len(FFN_SPLITS):
            ncol, nwidth = FFN_SPLITS[c + 1]
            nxt = (conv_up(ncol, nwidth), conv_up(FFN_HIDDEN + ncol, nwidth))
        act = (jax.nn.silu(gate) * val).astype(BF16)
        acc = acc + jnp.dot(act, wdn_ref[col:col + width, :], preferred_element_type=F32)
    o_ref[...] = _rms_norm(acc, fg_ref[...])


def _ffn(x2d, consts, batch, seq):
    m = x2d.shape[0]
    t = FFN_ROWS
    tiles = seq // t
    rows = pl.BlockSpec((t, D_MODEL), lambda b, s: (b * tiles + s, 0))
    return pl.pallas_call(
        _ffn_kernel,
        grid=(batch, tiles),
        in_specs=[rows] + [_const_spec(c.shape) for c in consts],
        out_specs=rows,
        out_shape=jax.ShapeDtypeStruct((m, D_MODEL), F32),
        scratch_shapes=[pltpu.VMEM((SUBLANES, 2 * FFN_HIDDEN), F32)],
        compiler_params=pltpu.CompilerParams(
            dimension_semantics=("arbitrary", "arbitrary"), vmem_limit_bytes=V7X_VMEM_LIMIT),
        name="convffn",
    )(x2d, *consts)


def _retention_decays():
    heads = jnp.arange(RET_HEADS, dtype=F32)
    log_g = jnp.log1p(-jnp.exp2(-5.0 - heads))
    pos = jnp.arange(RET_CHUNK, dtype=F32)
    rel = pos[:, None] - pos[None, :]
    d_in = jnp.where(rel >= 0, jnp.exp(log_g[:, None, None] * jnp.maximum(rel, 0.0)), 0.0)
    d_q = jnp.exp(log_g[:, None] * (pos + 1.0))
    d_k = jnp.exp(log_g[:, None] * (RET_CHUNK - 1.0 - pos))
    d_chunk = jnp.exp(log_g * RET_CHUNK)
    d_q = jnp.broadcast_to(d_q[:, :, None], (RET_HEADS, RET_CHUNK, RET_V_DIM))
    d_k = jnp.broadcast_to(d_k[:, :, None], (RET_HEADS, RET_CHUNK, RET_QK_DIM))
    return d_chunk, d_in, d_q, d_k


def kernel(x, in_norm_g, w_in, ret_gn_g, w_ret_o, lru_conv_w, lru_conv_b, lru_w_a, lru_b_a,
           lru_w_i, lru_b_i, lru_a_param, w_rnn_o, w_out, ffn_norm_g, w_up, ffn_conv_w,
           ffn_conv_b, w_down, final_norm_g):
    batch, seq, d = x.shape
    depth = w_in.shape[0]
    assert d == D_MODEL and seq % INPROJ_ROWS == 0 and seq % MIXER_ROWS == 0
    assert depth == 1, "the final norm is fused into the (single) layer's FFN kernel"

    pos = jnp.arange(seq, dtype=F32)
    inv_freq = ROPE_BASE ** (-jnp.arange(0, RET_QK_DIM, 2, dtype=F32) / RET_QK_DIM)
    ang = pos[:, None] * inv_freq[None, :]
    cos = jnp.cos(ang)
    sin = jnp.sin(ang)
    d_chunk, d_in, d_q, d_k = _retention_decays()

    x2d = x.reshape(batch * seq, d)
    for l in range(depth):
        inproj_consts = (
            in_norm_g[l][None, :], w_in[l].astype(BF16), cos, sin,
            lru_conv_w[l], lru_conv_b[l][None, :],
            lru_w_a[l].astype(BF16), lru_b_a[l].reshape(1, LRU_WIDTH),
            lru_w_i[l].astype(BF16), lru_b_i[l].reshape(1, LRU_WIDTH),
            lru_a_param[l][None, :],
        )
        proj = _inproj(x2d, inproj_consts, batch, seq)
        mixer_consts = (
            d_chunk, d_in, d_q, d_k,
            ret_gn_g[l][None, :], w_ret_o[l].astype(BF16),
            w_rnn_o[l].astype(BF16), w_out[l].astype(BF16),
        )
        x2d = _mixer(proj, x2d, mixer_consts, batch, seq)
        ffn_consts = (
            ffn_norm_g[l][None, :], w_up[l].astype(BF16), ffn_conv_w[l],
            ffn_conv_b[l][None, :], w_down[l].astype(BF16),
            final_norm_g[None, :],
        )
        x2d = _ffn(x2d, ffn_consts, batch, seq)
    return x2d.reshape(batch, seq, d)
```

```python
import functools
import math

import jax
import jax.numpy as jnp
from jax import lax
from jax.experimental import pallas as pl
from jax.experimental.pallas import tpu as pltpu

F32 = jnp.float32
BF16 = jnp.bfloat16

D_MODEL = 1024
RET_HEADS = 4
RET_QK_DIM = 256
RET_V_DIM = 512
RET_QK_WIDTH = RET_HEADS * RET_QK_DIM
RET_V_WIDTH = RET_HEADS * RET_V_DIM
RET_CHUNK = 256
ROPE_BASE = 10000.0
LRU_WIDTH = 1024
LRU_BLOCK = 256
LRU_BLOCKS = 4
LRU_CONV = 4
LRU_C = 8.0
FFN_HIDDEN = 2816
FFN_CONV = 3
NORM_EPS = 1e-6
GN_EPS = 1e-5
GELU_A = 2.0 * (2.0 / math.pi) ** 0.5
GELU_B = 0.044715 * GELU_A
TINY = 1e-30

SUBLANES = 8
LANES = 128
V7X_VMEM_LIMIT = 60 * 1024 * 1024

COL_Q = 0
COL_K = COL_Q + RET_QK_WIDTH
COL_V = COL_K + RET_QK_WIDTH
COL_G = COL_V + RET_V_WIDTH
COL_XR = COL_G + RET_V_WIDTH
COL_GR = COL_XR + LRU_WIDTH
COL_MRET = COL_GR + LRU_WIDTH
COL_MRNN = COL_MRET + D_MODEL
IN_WIDTH = COL_MRNN + D_MODEL

INPROJ_ROWS = 512
MM_COLS = 256
TAIL_ROWS = 64
FENCE_LAG = 2
MIXER_ROWS = 512
FFN_ROWS = 1024
FFN_SPLITS = ((0, 768), (768, 768), (1536, 768), (2304, 512))


def _const_spec(shape):
    nd = len(shape)
    return pl.BlockSpec(shape, lambda *_: (0,) * nd, pipeline_mode=pl.Buffered(1))


def _rms_norm(x, g):
    ms = jnp.mean(x * x, axis=-1, keepdims=True)
    return x * lax.rsqrt(ms + NORM_EPS) * g


def _gelu_tanh(x):
    two_z = x * (GELU_A + GELU_B * (x * x))
    return x / (1.0 + jnp.exp(-two_z))


def _shift_rows(cat, j, rows):
    return pltpu.roll(cat, j, axis=0)[SUBLANES:SUBLANES + rows, :]


def _inproj_kernel(x_ref, g_ref, w_ref, cos_ref, sin_ref,
                   cw_ref, cb_ref, wa_ref, ba_ref, wi_ref, bi_ref, ap_ref,
                   q_ref, k_ref, v_ref, sg_ref, hg_ref, sr_ref, sn_ref,
                   xcarry_ref, hcarry_ref, pre_r_sc, pre_i_sc, xc_sc, gg_sc, h_sc,
                   *, tiles_per_seq, num_tiles):
    rows = INPROJ_ROWS
    t = pl.program_id(0)

    @pl.when(t == 0)
    def _():
        pre_r_sc[...] = jnp.zeros_like(pre_r_sc)
        pre_i_sc[...] = jnp.zeros_like(pre_i_sc)
        xc_sc[...] = jnp.zeros_like(xc_sc)
        gg_sc[...] = jnp.zeros_like(gg_sc)

    @pl.when(jnp.minimum(t, num_tiles - 1) % tiles_per_seq == 0)
    def _():
        xcarry_ref[...] = jnp.zeros_like(xcarry_ref)

    @pl.when(jnp.maximum(t - 1, 0) % tiles_per_seq == 0)
    def _():
        hcarry_ref[...] = jnp.zeros_like(hcarry_ref)

    neg_c_softplus = -LRU_C * jax.nn.softplus(-ap_ref[...])
    first_row = lax.broadcasted_iota(jnp.int32, (SUBLANES, LRU_BLOCK), 0) == 0

    h_carry = [None] * LRU_BLOCKS

    def lru_tail(n, rc):
        cols = slice(n * LRU_BLOCK, (n + 1) * LRU_BLOCK)
        rs = slice(rc * TAIL_ROWS, (rc + 1) * TAIL_ROWS)
        r = jax.nn.sigmoid(pre_r_sc[rs, cols] + ba_ref[:, cols])
        gi = jax.nn.sigmoid(pre_i_sc[rs, cols] + bi_ref[:, cols])
        a_all = jnp.exp(neg_c_softplus[:, cols] * r)
        y = 1.0 - a_all * a_all
        u_all = y * lax.rsqrt(jnp.maximum(y, TINY)) * gi * xc_sc[rs, cols]
        h_prev = hcarry_ref[:, cols] if rc == 0 else h_carry[n]
        for j in range(TAIL_ROWS // SUBLANES):
            grp = slice(j * SUBLANES, (j + 1) * SUBLANES)
            a = a_all[grp, :]
            u = u_all[grp, :] + a * h_prev
            a = jnp.where(first_row, 0.0, a)
            for d in (1, 2):
                u = u + a * pltpu.roll(u, d, axis=0)
                a = a * pltpu.roll(a, d, axis=0)
            hj = u + a * pltpu.roll(u, 4, axis=0)
            out_rows = slice(rc * TAIL_ROWS + j * SUBLANES, rc * TAIL_ROWS + (j + 1) * SUBLANES)
            hg_ref[out_rows, cols] = (hj * gg_sc[out_rows, cols]).astype(BF16)
            h_prev = jnp.where(first_row, pltpu.roll(hj, 1, axis=0), 0.0)
        h_carry[n] = h_prev
        if (rc + 1) * TAIL_ROWS == rows:
            hcarry_ref[:, cols] = h_prev

    h_sc[0:rows, :] = _rms_norm(x_ref[...], g_ref[...]).astype(BF16)
    cos = cos_ref[...]
    sin = sin_ref[...]

    fence_row = rows + lax.shift_right_arithmetic(t, 31)

    def fence(hp):
        mark = jnp.concatenate([hp[:, :LANES], hp[:, LANES:2 * LANES]], axis=0).astype(BF16)
        h_sc[pl.ds(pl.multiple_of(fence_row, 2 * SUBLANES), 2 * SUBLANES), 0:LANES] = mark

    def mm(col):
        return jnp.dot(h_sc[0:rows, :], w_ref[:, col:col + MM_COLS], preferred_element_type=F32)

    cos_k = cos * RET_QK_DIM ** -0.5
    sin_k = sin * RET_QK_DIM ** -0.5

    def rotary(acc, out_ref, lo, c, s):
        half = RET_QK_DIM // 2
        t1 = acc[:, :half]
        t2 = acc[:, half:]
        out_ref[:, lo:lo + half] = (t1 * c - t2 * s).astype(BF16)
        out_ref[:, lo + half:lo + RET_QK_DIM] = (t1 * s + t2 * c).astype(BF16)

    def section(kind, i):
        lo = i * MM_COLS
        if kind == "q":
            rotary(mm(COL_Q + lo), q_ref, lo, cos, sin)
        elif kind == "k":
            rotary(mm(COL_K + lo), k_ref, lo, cos_k, sin_k)
        elif kind == "v":
            v_ref[:, lo:lo + MM_COLS] = mm(COL_V + lo).astype(BF16)
        elif kind == "g":
            sg_ref[:, lo:lo + MM_COLS] = jax.nn.silu(mm(COL_G + lo)).astype(BF16)
        elif kind == "mret":
            sr_ref[:, lo:lo + MM_COLS] = jax.nn.sigmoid(mm(COL_MRET + lo)).astype(BF16)
        else:
            sn_ref[:, lo:lo + MM_COLS] = jax.nn.sigmoid(mm(COL_MRNN + lo)).astype(BF16)

    def lru_front_a(n):
        cols = slice(n * LRU_BLOCK, (n + 1) * LRU_BLOCK)
        xr = mm(COL_XR + n * LRU_BLOCK)
        cat = jnp.concatenate([xcarry_ref[:, cols], xr], axis=0)
        xc = cb_ref[:, cols] + cw_ref[LRU_CONV - 1:LRU_CONV, cols] * xr
        for j in range(1, LRU_CONV):
            xc = xc + cw_ref[LRU_CONV - 1 - j:LRU_CONV - j, cols] * _shift_rows(cat, j, rows)
        xcarry_ref[:, cols] = xr[rows - SUBLANES:, :]
        xc_sc[:, cols] = xc
        return xc.astype(BF16)

    def lru_front_b(n, xcb):
        cols = slice(n * LRU_BLOCK, (n + 1) * LRU_BLOCK)
        gg_sc[:, cols] = _gelu_tanh(mm(COL_GR + n * LRU_BLOCK))
        pre_r_sc[:, cols] = jnp.dot(xcb, wa_ref[n], preferred_element_type=F32)
        pre_i_sc[:, cols] = jnp.dot(xcb, wi_ref[n], preferred_element_type=F32)

    tail_chunks = rows // TAIL_ROWS
    others = ([("v", i) for i in range(RET_V_WIDTH // MM_COLS)]
              + [("q", i) for i in range(RET_HEADS)] + [("k", i) for i in range(RET_HEADS)]
              + [("g", i) for i in range(RET_V_WIDTH // MM_COLS)]
              + [("mret", i) for i in range(D_MODEL // MM_COLS)]
              + [("mrnn", i) for i in range(D_MODEL // MM_COLS)])
    others.reverse()
    pending = []
    for n in range(LRU_BLOCKS):
        for rc in range(tail_chunks):
            lru_tail(n, rc)
            pending.append(h_carry[n])
            if len(pending) > FENCE_LAG:
                fence(pending.pop(0))
            if n == 0 or rc >= 2:
                section(*others.pop())
        xcb = lru_front_a(n)
        section(*others.pop())
        lru_front_b(n, xcb)
    while others:
        section(*others.pop())


def _inproj(x2d, consts, batch, seq):
    m = x2d.shape[0]
    tm = INPROJ_ROWS
    tiles = seq // tm
    num_tiles = m // tm

    def cur(width):
        return pl.BlockSpec((tm, width), lambda t: (jnp.minimum(t, num_tiles - 1), 0))

    def table():
        return pl.BlockSpec((tm, LANES), lambda t: (jnp.minimum(t, num_tiles - 1) % tiles, 0))

    prev = pl.BlockSpec((tm, LRU_WIDTH), lambda t: (jnp.maximum(t - 1, 0), 0))

    g, w_in, cos, sin = consts[:4]
    lru_consts = consts[4:]
    out_widths = (RET_QK_WIDTH, RET_QK_WIDTH, RET_V_WIDTH, RET_V_WIDTH,
                  LRU_WIDTH, D_MODEL, D_MODEL)
    out_specs = [cur(w) for w in out_widths]
    out_specs[4] = prev
    staged = pltpu.VMEM((tm, LRU_WIDTH), F32)
    return pl.pallas_call(
        functools.partial(_inproj_kernel, tiles_per_seq=tiles, num_tiles=num_tiles),
        grid=(num_tiles + 1,),
        in_specs=[
            cur(D_MODEL),
            _const_spec(g.shape),
            _const_spec(w_in.shape),
            table(),
            table(),
        ] + [_const_spec(c.shape) for c in lru_consts],
        out_specs=out_specs,
        out_shape=[jax.ShapeDtypeStruct((m, w), BF16) for w in out_widths],
        scratch_shapes=[
            pltpu.VMEM((SUBLANES, LRU_WIDTH), F32),
            pltpu.VMEM((SUBLANES, LRU_WIDTH), F32),
            staged, staged, staged, staged,
            pltpu.VMEM((tm + 2 * SUBLANES, D_MODEL), BF16),
        ],
        compiler_params=pltpu.CompilerParams(
            dimension_semantics=("arbitrary",), vmem_limit_bytes=V7X_VMEM_LIMIT),
        name="inproj",
    )(x2d, g, w_in, cos, sin, *lru_consts)


def _mixer_kernel(q_ref, k_ref, v_ref, sg_ref, hg_ref, sr_ref, sn_ref, x_ref,
                  dch_ref, din_ref, dq_ref, dk_ref, gn_ref, wro_ref, wrn_ref, wout_ref,
                  o_ref,
                  state_ref, go_ref):
    rows = MIXER_ROWS

    @pl.when(pl.program_id(1) == 0)
    def _():
        state_ref[...] = jnp.zeros_like(state_ref)

    for c in range(rows // RET_CHUNK):
        r0 = c * RET_CHUNK
        for hd in range(RET_HEADS):
            qk = slice(hd * RET_QK_DIM, (hd + 1) * RET_QK_DIM)
            vv = slice(hd * RET_V_DIM, (hd + 1) * RET_V_DIM)
            qh = q_ref[r0:r0 + RET_CHUNK, qk]
            kh = k_ref[r0:r0 + RET_CHUNK, qk]
            vh = v_ref[r0:r0 + RET_CHUNK, vv]
            scores = lax.dot_general(qh, kh, (((1,), (1,)), ((), ())),
                                     preferred_element_type=F32)
            scores = (scores * din_ref[hd]).astype(BF16)
            st = state_ref[hd]
            o = (jnp.dot(scores, vh, preferred_element_type=F32)
                 + jnp.dot(qh, st.astype(BF16), preferred_element_type=F32) * dq_ref[hd])
            kd = (kh.astype(F32) * dk_ref[hd]).astype(BF16)
            state_ref[hd] = st * dch_ref[hd] + lax.dot_general(
                kd, vh, (((0,), (0,)), ((), ())), preferred_element_type=F32)
            mu = jnp.mean(o, axis=-1, keepdims=True)
            dev = o - mu
            var = jnp.mean(dev * dev, axis=-1, keepdims=True)
            y = dev * lax.rsqrt(var + GN_EPS) * gn_ref[:, vv]
            go_ref[r0:r0 + RET_CHUNK, vv] = (
                y * sg_ref[r0:r0 + RET_CHUNK, vv].astype(F32)).astype(BF16)
    y_ret = jnp.dot(go_ref[...], wro_ref[...], preferred_element_type=F32)
    y_rnn = jnp.dot(hg_ref[...], wrn_ref[...], preferred_element_type=F32)

    mixed = (sr_ref[...].astype(F32) * y_ret + sn_ref[...].astype(F32) * y_rnn).astype(BF16)
    o_ref[...] = x_ref[...] + jnp.dot(mixed, wout_ref[...], preferred_element_type=F32)


def _mixer(proj, x2d, consts, batch, seq):
    q, k, v, sg, hg, sr, sn = proj
    m = x2d.shape[0]
    t = MIXER_ROWS
    tiles = seq // t

    def rows(width):
        return pl.BlockSpec((t, width), lambda b, s: (b * tiles + s, 0))

    const_specs = [pl.BlockSpec(memory_space=pltpu.SMEM)]
    const_specs += [_const_spec(c.shape) for c in consts[1:]]
    return pl.pallas_call(
        _mixer_kernel,
        grid=(batch, tiles),
        in_specs=[rows(RET_QK_WIDTH), rows(RET_QK_WIDTH), rows(RET_V_WIDTH), rows(RET_V_WIDTH),
                  rows(LRU_WIDTH), rows(D_MODEL), rows(D_MODEL), rows(D_MODEL)]
                 + const_specs,
        out_specs=rows(D_MODEL),
        out_shape=jax.ShapeDtypeStruct((m, D_MODEL), F32),
        scratch_shapes=[
            pltpu.VMEM((RET_HEADS, RET_QK_DIM, RET_V_DIM), F32),
            pltpu.VMEM((t, RET_V_WIDTH), BF16),
        ],
        compiler_params=pltpu.CompilerParams(
            dimension_semantics=("arbitrary", "arbitrary"), vmem_limit_bytes=V7X_VMEM_LIMIT),
        name="mixer",
    )(q, k, v, sg, hg, sr, sn, x2d, *consts)


def _ffn_kernel(x_ref, g_ref, wup_ref, cw_ref, cb_ref, wdn_ref, fg_ref, o_ref, carry_ref):
    rows = FFN_ROWS

    @pl.when(pl.program_id(1) == 0)
    def _():
        carry_ref[...] = jnp.zeros_like(carry_ref)

    x = x_ref[...]
    h = _rms_norm(x, g_ref[...]).astype(BF16)

    def conv_up(col, width):
        cols = slice(col, col + width)
        up = jnp.dot(h, wup_ref[:, cols], preferred_element_type=F32)
        cat = jnp.concatenate([carry_ref[:, cols], up], axis=0)
        y = cb_ref[:, cols] + cw_ref[FFN_CONV - 1:FFN_CONV, cols] * up
        for j in range(1, FFN_CONV):
            y = y + cw_ref[FFN_CONV - 1 - j:FFN_CONV - j, cols] * _shift_rows(cat, j, rows)
        carry_ref[:, cols] = up[rows - SUBLANES:, :]
        return y

    acc = x
    nxt = (conv_up(FFN_SPLITS[0][0], FFN_SPLITS[0][1]),
           conv_up(FFN_HIDDEN + FFN_SPLITS[0][0], FFN_SPLITS[0][1]))
    for c, (col, width) in enumerate(FFN_SPLITS):
        gate, val = nxt
        if c + 1 < len(FFN_SPLITS):
            ncol, nwidth = FFN_SPLITS[c + 1]
            nxt = (conv_up(ncol, nwidth), conv_up(FFN_HIDDEN + ncol, nwidth))
        act = (jax.nn.silu(gate) * val).astype(BF16)
        acc = acc + jnp.dot(act, wdn_ref[col:col + width, :], preferred_element_type=F32)
    o_ref[...] = _rms_norm(acc, fg_ref[...])


def _ffn(x2d, consts, batch, seq):
    m = x2d.shape[0]
    t = FFN_ROWS
    tiles = seq // t
    rows = pl.BlockSpec((t, D_MODEL), lambda b, s: (b * tiles + s, 0))
    return pl.pallas_call(
        _ffn_kernel,
        grid=(batch, tiles),
        in_specs=[rows] + [_const_spec(c.shape) for c in consts],
        out_specs=rows,
        out_shape=jax.ShapeDtypeStruct((m, D_MODEL), F32),
        scratch_shapes=[pltpu.VMEM((SUBLANES, 2 * FFN_HIDDEN), F32)],
        compiler_params=pltpu.CompilerParams(
            dimension_semantics=("arbitrary", "arbitrary"), vmem_limit_bytes=V7X_VMEM_LIMIT),
        name="convffn",
    )(x2d, *consts)


def _retention_decays():
    heads = jnp.arange(RET_HEADS, dtype=F32)
    log_g = jnp.log1p(-jnp.exp2(-5.0 - heads))
    pos = jnp.arange(RET_CHUNK, dtype=F32)
    rel = pos[:, None] - pos[None, :]
    d_in = jnp.where(rel >= 0, jnp.exp(log_g[:, None, None] * jnp.maximum(rel, 0.0)), 0.0)
    d_q = jnp.exp(log_g[:, None] * (pos + 1.0))
    d_k = jnp.exp(log_g[:, None] * (RET_CHUNK - 1.0 - pos))
    d_chunk = jnp.exp(log_g * RET_CHUNK)
    d_q = jnp.broadcast_to(d_q[:, :, None], (RET_HEADS, RET_CHUNK, RET_V_DIM))
    d_k = jnp.broadcast_to(d_k[:, :, None], (RET_HEADS, RET_CHUNK, RET_QK_DIM))
    return d_chunk, d_in, d_q, d_k


def kernel(x, in_norm_g, w_in, ret_gn_g, w_ret_o, lru_conv_w, lru_conv_b, lru_w_a, lru_b_a,
           lru_w_i, lru_b_i, lru_a_param, w_rnn_o, w_out, ffn_norm_g, w_up, ffn_conv_w,
           ffn_conv_b, w_down, final_norm_g):
    batch, seq, d = x.shape
    depth = w_in.shape[0]
    assert d == D_MODEL and seq % INPROJ_ROWS == 0 and seq % MIXER_ROWS == 0 and seq % FFN_ROWS == 0
    assert depth == 1, "the final norm is fused into the (single) layer's FFN kernel"

    pos = jnp.arange(seq, dtype=F32)
    inv_freq = ROPE_BASE ** (-jnp.arange(0, RET_QK_DIM, 2, dtype=F32) / RET_QK_DIM)
    ang = pos[:, None] * inv_freq[None, :]
    cos = jnp.cos(ang)
    sin = jnp.sin(ang)
    d_chunk, d_in, d_q, d_k = _retention_decays()

    x2d = x.reshape(batch * seq, d)
    for l in range(depth):
        inproj_consts = (
            in_norm_g[l][None, :], w_in[l].astype(BF16), cos, sin,
            lru_conv_w[l], lru_conv_b[l][None, :],
            lru_w_a[l].astype(BF16), lru_b_a[l].reshape(1, LRU_WIDTH),
            lru_w_i[l].astype(BF16), lru_b_i[l].reshape(1, LRU_WIDTH),
            lru_a_param[l][None, :],
        )
        proj = _inproj(x2d, inproj_consts, batch, seq)
        mixer_consts = (
            d_chunk, d_in, d_q, d_k,
            ret_gn_g[l][None, :], w_ret_o[l].astype(BF16),
            w_rnn_o[l].astype(BF16), w_out[l].astype(BF16),
        )
        x2d = _mixer(proj, x2d, mixer_consts, batch, seq)
        ffn_consts = (
            ffn_norm_g[l][None, :], w_up[l].astype(BF16), ffn_conv_w[l],
            ffn_conv_b[l][None, :], w_down[l].astype(BF16),
            final_norm_g[None, :],
        )
        x2d = _ffn(x2d, ffn_consts, batch, seq)
    return x2d.reshape(batch, seq, d)
```

```python
import functools
import math

import jax
import jax.numpy as jnp
from jax import lax
from jax.experimental import pallas as pl
from jax.experimental.pallas import tpu as pltpu

F32 = jnp.float32
BF16 = jnp.bfloat16

D_MODEL = 1024
RET_HEADS = 4
RET_QK_DIM = 256
RET_V_DIM = 512
RET_QK_WIDTH = RET_HEADS * RET_QK_DIM
RET_V_WIDTH = RET_HEADS * RET_V_DIM
RET_CHUNK = 256
ROPE_BASE = 10000.0
LRU_WIDTH = 1024
LRU_BLOCK = 256
LRU_BLOCKS = 4
LRU_CONV = 4
LRU_C = 8.0
FFN_HIDDEN = 2816
FFN_CONV = 3
NORM_EPS = 1e-6
GN_EPS = 1e-5
GELU_A = 2.0 * (2.0 / math.pi) ** 0.5
GELU_B = 0.044715 * GELU_A
TINY = 1e-30

SUBLANES = 8
LANES = 128
V7X_VMEM_LIMIT = 60 * 1024 * 1024

COL_Q = 0
COL_K = COL_Q + RET_QK_WIDTH
COL_V = COL_K + RET_QK_WIDTH
COL_G = COL_V + RET_V_WIDTH
COL_XR = COL_G + RET_V_WIDTH
COL_GR = COL_XR + LRU_WIDTH
COL_MRET = COL_GR + LRU_WIDTH
COL_MRNN = COL_MRET + D_MODEL
IN_WIDTH = COL_MRNN + D_MODEL

INPROJ_ROWS = 512
MM_COLS = 256
TAIL_ROWS = 128
PIECES_PER_BLOCK = 8
FENCE_LAG = 1
MIXER_ROWS = 512
FFN_ROWS = 512
FFN_SPLITS = ((0, 768), (768, 768), (1536, 768), (2304, 512))


def _const_spec(shape):
    nd = len(shape)
    return pl.BlockSpec(shape, lambda *_: (0,) * nd, pipeline_mode=pl.Buffered(1))


def _rms_norm(x, g):
    ms = jnp.mean(x * x, axis=-1, keepdims=True)
    return x * lax.rsqrt(ms + NORM_EPS) * g


def _gelu_tanh(x):
    two_z = x * (GELU_A + GELU_B * (x * x))
    return x / (1.0 + jnp.exp(-two_z))


def _shift_rows(cat, j, rows):
    return pltpu.roll(cat, j, axis=0)[SUBLANES:SUBLANES + rows, :]


def _inproj_kernel(x_ref, g_ref, w_ref, cos_ref, sin_ref,
                   cw_ref, cb_ref, wa_ref, ba_ref, wi_ref, bi_ref, ap_ref,
                   q_ref, k_ref, v_ref, sg_ref, hg_ref, sr_ref, sn_ref,
                   xcarry_ref, hcarry_ref, pre_r_sc, pre_i_sc, xc_sc, gg_sc, h_sc,
                   *, tiles_per_seq, num_tiles):
    rows = INPROJ_ROWS
    t = pl.program_id(0)

    @pl.when(t == 0)
    def _():
        pre_r_sc[...] = jnp.zeros_like(pre_r_sc)
        pre_i_sc[...] = jnp.zeros_like(pre_i_sc)
        xc_sc[...] = jnp.zeros_like(xc_sc)
        gg_sc[...] = jnp.zeros_like(gg_sc)

    @pl.when(jnp.minimum(t, num_tiles - 1) % tiles_per_seq == 0)
    def _():
        xcarry_ref[...] = jnp.zeros_like(xcarry_ref)

    @pl.when(jnp.maximum(t - 1, 0) % tiles_per_seq == 0)
    def _():
        hcarry_ref[...] = jnp.zeros_like(hcarry_ref)

    neg_c_softplus = -LRU_C * jax.nn.softplus(-ap_ref[...])
    first_row = lax.broadcasted_iota(jnp.int32, (SUBLANES, LRU_BLOCK), 0) == 0

    h_carry = [None] * LRU_BLOCKS

    def lru_tail(n, rc):
        cols = slice(n * LRU_BLOCK, (n + 1) * LRU_BLOCK)
        rs = slice(rc * TAIL_ROWS, (rc + 1) * TAIL_ROWS)
        r = jax.nn.sigmoid(pre_r_sc[rs, cols] + ba_ref[:, cols])
        gi = jax.nn.sigmoid(pre_i_sc[rs, cols] + bi_ref[:, cols])
        a_all = jnp.exp(neg_c_softplus[:, cols] * r)
        y = 1.0 - a_all * a_all
        u_all = y * lax.rsqrt(jnp.maximum(y, TINY)) * gi * xc_sc[rs, cols]
        h_prev = hcarry_ref[:, cols] if rc == 0 else h_carry[n]
        for j in range(TAIL_ROWS // SUBLANES):
            grp = slice(j * SUBLANES, (j + 1) * SUBLANES)
            a = a_all[grp, :]
            u = u_all[grp, :] + a * h_prev
            a = jnp.where(first_row, 0.0, a)
            for d in (1, 2):
                u = u + a * pltpu.roll(u, d, axis=0)
                a = a * pltpu.roll(a, d, axis=0)
            hj = u + a * pltpu.roll(u, 4, axis=0)
            out_rows = slice(rc * TAIL_ROWS + j * SUBLANES, rc * TAIL_ROWS + (j + 1) * SUBLANES)
            hg_ref[out_rows, cols] = (hj * gg_sc[out_rows, cols]).astype(BF16)
            h_prev = jnp.where(first_row, pltpu.roll(hj, 1, axis=0), 0.0)
        h_carry[n] = h_prev
        if (rc + 1) * TAIL_ROWS == rows:
            hcarry_ref[:, cols] = h_prev

    h_sc[0:rows, :] = _rms_norm(x_ref[...], g_ref[...]).astype(BF16)
    cos = cos_ref[...]
    sin = sin_ref[...]

    fence_row = rows + lax.shift_right_arithmetic(t, 31)

    def fence(hp):
        mark = jnp.concatenate([hp[:, :LANES], hp[:, LANES:2 * LANES]], axis=0).astype(BF16)
        h_sc[pl.ds(pl.multiple_of(fence_row, 2 * SUBLANES), 2 * SUBLANES), 0:LANES] = mark

    def mm(col):
        return jnp.dot(h_sc[0:rows, :], w_ref[:, col:col + MM_COLS], preferred_element_type=F32)

    cos_k = cos * RET_QK_DIM ** -0.5
    sin_k = sin * RET_QK_DIM ** -0.5

    def rotary(acc, out_ref, lo, c, s):
        half = RET_QK_DIM // 2
        t1 = acc[:, :half]
        t2 = acc[:, half:]
        out_ref[:, lo:lo + half] = (t1 * c - t2 * s).astype(BF16)
        out_ref[:, lo + half:lo + RET_QK_DIM] = (t1 * s + t2 * c).astype(BF16)

    def section(kind, i):
        lo = i * MM_COLS
        if kind == "q":
            rotary(mm(COL_Q + lo), q_ref, lo, cos, sin)
        elif kind == "k":
            rotary(mm(COL_K + lo), k_ref, lo, cos_k, sin_k)
        elif kind == "v":
            v_ref[:, lo:lo + MM_COLS] = mm(COL_V + lo).astype(BF16)
        elif kind == "g":
            sg_ref[:, lo:lo + MM_COLS] = jax.nn.silu(mm(COL_G + lo)).astype(BF16)
        elif kind == "mret":
            sr_ref[:, lo:lo + MM_COLS] = jax.nn.sigmoid(mm(COL_MRET + lo)).astype(BF16)
        else:
            sn_ref[:, lo:lo + MM_COLS] = jax.nn.sigmoid(mm(COL_MRNN + lo)).astype(BF16)

    def lru_front_a(n):
        cols = slice(n * LRU_BLOCK, (n + 1) * LRU_BLOCK)
        xr = mm(COL_XR + n * LRU_BLOCK)
        cat = jnp.concatenate([xcarry_ref[:, cols], xr], axis=0)
        xc = cb_ref[:, cols] + cw_ref[LRU_CONV - 1:LRU_CONV, cols] * xr
        for j in range(1, LRU_CONV):
            xc = xc + cw_ref[LRU_CONV - 1 - j:LRU_CONV - j, cols] * _shift_rows(cat, j, rows)
        xcarry_ref[:, cols] = xr[rows - SUBLANES:, :]
        xc_sc[:, cols] = xc
        return xc.astype(BF16)

    def lru_front_b(n, xcb):
        cols = slice(n * LRU_BLOCK, (n + 1) * LRU_BLOCK)
        gg_sc[:, cols] = _gelu_tanh(mm(COL_GR + n * LRU_BLOCK))
        pre_r_sc[:, cols] = jnp.dot(xcb, wa_ref[n], preferred_element_type=F32)
        pre_i_sc[:, cols] = jnp.dot(xcb, wi_ref[n], preferred_element_type=F32)

    tail_chunks = rows // TAIL_ROWS
    others = ([("v", i) for i in range(RET_V_WIDTH // MM_COLS)]
              + [("q", i) for i in range(RET_HEADS)] + [("k", i) for i in range(RET_HEADS)]
              + [("g", i) for i in range(RET_V_WIDTH // MM_COLS)]
              + [("mret", i) for i in range(D_MODEL // MM_COLS)]
              + [("mrnn", i) for i in range(D_MODEL // MM_COLS)])
    others.reverse()
    pending = []
    for n in range(LRU_BLOCKS):
        pieces = PIECES_PER_BLOCK if n == 0 else PIECES_PER_BLOCK - 2
        issued = 0
        for rc in range(tail_chunks):
            lru_tail(n, rc)
            pending.append(h_carry[n])
            if len(pending) > FENCE_LAG:
                fence(pending.pop(0))
            due = ((rc + 1) * pieces) // tail_chunks
            while issued < due:
                section(*others.pop())
                issued += 1
        xcb = lru_front_a(n)
        section(*others.pop())
        lru_front_b(n, xcb)
    while others:
        section(*others.pop())


def _inproj(x2d, consts, batch, seq):
    m = x2d.shape[0]
    tm = INPROJ_ROWS
    tiles = seq // tm
    num_tiles = m // tm

    def cur(width):
        return pl.BlockSpec((tm, width), lambda t: (jnp.minimum(t, num_tiles - 1), 0))

    def table():
        return pl.BlockSpec((tm, LANES), lambda t: (jnp.minimum(t, num_tiles - 1) % tiles, 0))

    prev = pl.BlockSpec((tm, LRU_WIDTH), lambda t: (jnp.maximum(t - 1, 0), 0))

    g, w_in, cos, sin = consts[:4]
    lru_consts = consts[4:]
    out_widths = (RET_QK_WIDTH, RET_QK_WIDTH, RET_V_WIDTH, RET_V_WIDTH,
                  LRU_WIDTH, D_MODEL, D_MODEL)
    out_specs = [cur(w) for w in out_widths]
    out_specs[4] = prev
    staged = pltpu.VMEM((tm, LRU_WIDTH), F32)
    return pl.pallas_call(
        functools.partial(_inproj_kernel, tiles_per_seq=tiles, num_tiles=num_tiles),
        grid=(num_tiles + 1,),
        in_specs=[
            cur(D_MODEL),
            _const_spec(g.shape),
            _const_spec(w_in.shape),
            table(),
            table(),
        ] + [_const_spec(c.shape) for c in lru_consts],
        out_specs=out_specs,
        out_shape=[jax.ShapeDtypeStruct((m, w), BF16) for w in out_widths],
        scratch_shapes=[
            pltpu.VMEM((SUBLANES, LRU_WIDTH), F32),
            pltpu.VMEM((SUBLANES, LRU_WIDTH), F32),
            staged, staged, staged, staged,
            pltpu.VMEM((tm + 2 * SUBLANES, D_MODEL), BF16),
        ],
        compiler_params=pltpu.CompilerParams(
            dimension_semantics=("arbitrary",), vmem_limit_bytes=V7X_VMEM_LIMIT),
        name="inproj",
    )(x2d, g, w_in, cos, sin, *lru_consts)


def _mixer_kernel(q_ref, k_ref, v_ref, sg_ref, hg_ref, sr_ref, sn_ref, x_ref,
                  dch_ref, din_ref, dq_ref, dk_ref, gn_ref, wro_ref, wrn_ref, wout_ref,
                  o_ref,
                  state_ref, go_ref):
    rows = MIXER_ROWS

    @pl.when(pl.program_id(1) == 0)
    def _():
        state_ref[...] = jnp.zeros_like(state_ref)

    y_rnn = jnp.dot(hg_ref[...], wrn_ref[...], preferred_element_type=F32)

    for c in range(rows // RET_CHUNK):
        r0 = c * RET_CHUNK
        for hd in range(RET_HEADS):
            qk = slice(hd * RET_QK_DIM, (hd + 1) * RET_QK_DIM)
            vv = slice(hd * RET_V_DIM, (hd + 1) * RET_V_DIM)
            qh = q_ref[r0:r0 + RET_CHUNK, qk]
            kh = k_ref[r0:r0 + RET_CHUNK, qk]
            vh = v_ref[r0:r0 + RET_CHUNK, vv]
            scores = lax.dot_general(qh, kh, (((1,), (1,)), ((), ())),
                                     preferred_element_type=F32)
            scores = (scores * din_ref[hd]).astype(BF16)
            st = state_ref[hd]
            o = (jnp.dot(scores, vh, preferred_element_type=F32)
                 + jnp.dot(qh, st.astype(BF16), preferred_element_type=F32) * dq_ref[hd])
            kd = (kh.astype(F32) * dk_ref[hd]).astype(BF16)
            state_ref[hd] = st * dch_ref[hd] + lax.dot_general(
                kd, vh, (((0,), (0,)), ((), ())), preferred_element_type=F32)
            mu = jnp.mean(o, axis=-1, keepdims=True)
            dev = o - mu
            var = jnp.mean(dev * dev, axis=-1, keepdims=True)
            y = dev * lax.rsqrt(var + GN_EPS) * gn_ref[:, vv]
            go_ref[r0:r0 + RET_CHUNK, vv] = (
                y * sg_ref[r0:r0 + RET_CHUNK, vv].astype(F32)).astype(BF16)
    y_ret = jnp.dot(go_ref[...], wro_ref[...], preferred_element_type=F32)

    mixed = (sr_ref[...].astype(F32) * y_ret + sn_ref[...].astype(F32) * y_rnn).astype(BF16)
    o_ref[...] = x_ref[...] + jnp.dot(mixed, wout_ref[...], preferred_element_type=F32)


def _mixer(proj, x2d, consts, batch, seq):
    q, k, v, sg, hg, sr, sn = proj
    m = x2d.shape[0]
    t = MIXER_ROWS
    tiles = seq // t

    def rows(width):
        return pl.BlockSpec((t, width), lambda b, s: (b * tiles + s, 0))

    const_specs = [pl.BlockSpec(memory_space=pltpu.SMEM)]
    const_specs += [_const_spec(c.shape) for c in consts[1:]]
    return pl.pallas_call(
        _mixer_kernel,
        grid=(batch, tiles),
        in_specs=[rows(RET_QK_WIDTH), rows(RET_QK_WIDTH), rows(RET_V_WIDTH), rows(RET_V_WIDTH),
                  rows(LRU_WIDTH), rows(D_MODEL), rows(D_MODEL), rows(D_MODEL)]
                 + const_specs,
        out_specs=rows(D_MODEL),
        out_shape=jax.ShapeDtypeStruct((m, D_MODEL), F32),
        scratch_shapes=[
            pltpu.VMEM((RET_HEADS, RET_QK_DIM, RET_V_DIM), F32),
            pltpu.VMEM((t, RET_V_WIDTH), BF16),
        ],
        compiler_params=pltpu.CompilerParams(
            dimension_semantics=("arbitrary", "arbitrary"), vmem_limit_bytes=V7X_VMEM_LIMIT),
        name="mixer",
    )(q, k, v, sg, hg, sr, sn, x2d, *consts)


def _ffn_kernel(x_ref, g_ref, wup_ref, cw_ref, cb_ref, wdn_ref, fg_ref, o_ref, carry_ref):
    rows = FFN_ROWS

    @pl.when(pl.program_id(1) == 0)
    def _():
        carry_ref[...] = jnp.zeros_like(carry_ref)

    x = x_ref[...]
    h = _rms_norm(x, g_ref[...]).astype(BF16)

    def conv_up(col, width):
        cols = slice(col, col + width)
        up = jnp.dot(h, wup_ref[:, cols], preferred_element_type=F32)
        cat = jnp.concatenate([carry_ref[:, cols], up], axis=0)
        y = cb_ref[:, cols] + cw_ref[FFN_CONV - 1:FFN_CONV, cols] * up
        for j in range(1, FFN_CONV):
            y = y + cw_ref[FFN_CONV - 1 - j:FFN_CONV - j, cols] * _shift_rows(cat, j, rows)
        carry_ref[:, cols] = up[rows - SUBLANES:, :]
        return y

    acc = x
    nxt = (conv_up(FFN_SPLITS[0][0], FFN_SPLITS[0][1]),
           conv_up(FFN_HIDDEN + FFN_SPLITS[0][0], FFN_SPLITS[0][1]))
    for c, (col, width) in enumerate(FFN_SPLITS):
        gate, val = nxt
        if c + 1 < len(FFN_SPLITS):
            ncol, nwidth = FFN_SPLITS[c + 1]
            nxt = (conv_up(ncol, nwidth), conv_up(FFN_HIDDEN + ncol, nwidth))
        act = (jax.nn.silu(gate) * val).astype(BF16)
        acc = acc + jnp.dot(act, wdn_ref[col:col + width, :], preferred_element_type=F32)
    o_ref[...] = _rms_norm(acc, fg_ref[...])


def _ffn(x2d, consts, batch, seq):
    m = x2d.shape[0]
    t = FFN_ROWS
    tiles = seq // t
    rows = pl.BlockSpec((t, D_MODEL), lambda b, s: (b * tiles + s, 0))
    return pl.pallas_call(
        _ffn_kernel,
        grid=(batch, tiles),
        in_specs=[rows] + [_const_spec(c.shape) for c in consts],
        out_specs=rows,
        out_shape=jax.ShapeDtypeStruct((m, D_MODEL), F32),
        scratch_shapes=[pltpu.VMEM((SUBLANES, 2 * FFN_HIDDEN), F32)],
        compiler_params=pltpu.CompilerParams(
            dimension_semantics=("arbitrary", "arbitrary"), vmem_limit_bytes=V7X_VMEM_LIMIT),
        name="convffn",
    )(x2d, *consts)


def _retention_decays():
    heads = jnp.arange(RET_HEADS, dtype=F32)
    log_g = jnp.log1p(-jnp.exp2(-5.0 - heads))
    pos = jnp.arange(RET_CHUNK, dtype=F32)
    rel = pos[:, None] - pos[None, :]
    d_in = jnp.where(rel >= 0, jnp.exp(log_g[:, None, None] * jnp.maximum(rel, 0.0)), 0.0)
    d_q = jnp.exp(log_g[:, None] * (pos + 1.0))
    d_k = jnp.exp(log_g[:, None] * (RET_CHUNK - 1.0 - pos))
    d_chunk = jnp.exp(log_g * RET_CHUNK)
    d_q = jnp.broadcast_to(d_q[:, :, None], (RET_HEADS, RET_CHUNK, RET_V_DIM))
    d_k = jnp.broadcast_to(d_k[:, :, None], (RET_HEADS, RET_CHUNK, RET_QK_DIM))
    return d_chunk, d_in, d_q, d_k


def kernel(x, in_norm_g, w_in, ret_gn_g, w_ret_o, lru_conv_w, lru_conv_b, lru_w_a, lru_b_a,
           lru_w_i, lru_b_i, lru_a_param, w_rnn_o, w_out, ffn_norm_g, w_up, ffn_conv_w,
           ffn_conv_b, w_down, final_norm_g):
    batch, seq, d = x.shape
    depth = w_in.shape[0]
    assert d == D_MODEL and seq % INPROJ_ROWS == 0 and seq % MIXER_ROWS == 0 and seq % FFN_ROWS == 0
    assert depth == 1, "the final norm is fused into the (single) layer's FFN kernel"

    pos = jnp.arange(seq, dtype=F32)
    inv_freq = ROPE_BASE ** (-jnp.arange(0, RET_QK_DIM, 2, dtype=F32) / RET_QK_DIM)
    ang = pos[:, None] * inv_freq[None, :]
    cos = jnp.cos(ang)
    sin = jnp.sin(ang)
    d_chunk, d_in, d_q, d_k = _retention_decays()

    x2d = x.reshape(batch * seq, d)
    for l in range(depth):
        inproj_consts = (
            in_norm_g[l][None, :], w_in[l].astype(BF16), cos, sin,
            lru_conv_w[l], lru_conv_b[l][None, :],
            lru_w_a[l].astype(BF16), lru_b_a[l].reshape(1, LRU_WIDTH),
            lru_w_i[l].astype(BF16), lru_b_i[l].reshape(1, LRU_WIDTH),
            lru_a_param[l][None, :],
        )
        proj = _inproj(x2d, inproj_consts, batch, seq)
        mixer_consts = (
            d_chunk, d_in, d_q, d_k,
            ret_gn_g[l][None, :], w_ret_o[l].astype(BF16),
            w_rnn_o[l].astype(BF16), w_out[l].astype(BF16),
        )
        x2d = _mixer(proj, x2d, mixer_consts, batch, seq)
        ffn_consts = (
            ffn_norm_g[l][None, :], w_up[l].astype(BF16), ffn_conv_w[l],
            ffn_conv_b[l][None, :], w_down[l].astype(BF16),
            final_norm_g[None, :],
        )
        x2d = _ffn(x2d, ffn_consts, batch, seq)
    return x2d.reshape(batch, seq, d)
```

```python
import functools
import math

import jax
import jax.numpy as jnp
from jax import lax
from jax.experimental import pallas as pl
from jax.experimental.pallas import tpu as pltpu

F32 = jnp.float32
BF16 = jnp.bfloat16

D_MODEL = 1024
RET_HEADS = 4
RET_QK_DIM = 256
RET_V_DIM = 512
RET_QK_WIDTH = RET_HEADS * RET_QK_DIM
RET_V_WIDTH = RET_HEADS * RET_V_DIM
RET_CHUNK = 256
ROPE_BASE = 10000.0
LRU_WIDTH = 1024
LRU_BLOCK = 256
LRU_BLOCKS = 4
LRU_CONV = 4
LRU_C = 8.0
FFN_HIDDEN = 2816
FFN_CONV = 3
NORM_EPS = 1e-6
GN_EPS = 1e-5
GELU_A = 2.0 * (2.0 / math.pi) ** 0.5
GELU_B = 0.044715 * GELU_A
TINY = 1e-30

SUBLANES = 8
LANES = 128
V7X_VMEM_LIMIT = 60 * 1024 * 1024

COL_Q = 0
COL_K = COL_Q + RET_QK_WIDTH
COL_V = COL_K + RET_QK_WIDTH
COL_G = COL_V + RET_V_WIDTH
COL_XR = COL_G + RET_V_WIDTH
COL_GR = COL_XR + LRU_WIDTH
COL_MRET = COL_GR + LRU_WIDTH
COL_MRNN = COL_MRET + D_MODEL
IN_WIDTH = COL_MRNN + D_MODEL

INPROJ_ROWS = 512
MM_COLS = 256
TAIL_ROWS = 128
PIECES_PER_BLOCK = 8
FENCE_EVERY = 2
FENCE_LAG = 0
MIXER_ROWS = 512
FFN_ROWS = 512
FFN_SPLITS = ((0, 768), (768, 768), (1536, 768), (2304, 512))


def _const_spec(shape):
    nd = len(shape)
    return pl.BlockSpec(shape, lambda *_: (0,) * nd, pipeline_mode=pl.Buffered(1))


def _rms_norm(x, g):
    ms = jnp.mean(x * x, axis=-1, keepdims=True)
    return x * lax.rsqrt(ms + NORM_EPS) * g


def _gelu_tanh(x):
    two_z = x * (GELU_A + GELU_B * (x * x))
    return x / (1.0 + jnp.exp(-two_z))


def _shift_rows(cat, j, rows):
    return pltpu.roll(cat, j, axis=0)[SUBLANES:SUBLANES + rows, :]


def _inproj_kernel(x_ref, g_ref, w_ref, cos_ref, sin_ref,
                   cw_ref, cb_ref, wa_ref, ba_ref, wi_ref, bi_ref, ap_ref,
                   q_ref, k_ref, v_ref, sg_ref, hg_ref, sr_ref, sn_ref,
                   xcarry_ref, hcarry_ref, pre_r_sc, pre_i_sc, xc_sc, gg_sc, h_sc,
                   *, tiles_per_seq, num_tiles):
    rows = INPROJ_ROWS
    t = pl.program_id(0)

    @pl.when(t == 0)
    def _():
        pre_r_sc[...] = jnp.zeros_like(pre_r_sc)
        pre_i_sc[...] = jnp.zeros_like(pre_i_sc)
        xc_sc[...] = jnp.zeros_like(xc_sc)
        gg_sc[...] = jnp.zeros_like(gg_sc)

    @pl.when(jnp.minimum(t, num_tiles - 1) % tiles_per_seq == 0)
    def _():
        xcarry_ref[...] = jnp.zeros_like(xcarry_ref)

    @pl.when(jnp.maximum(t - 1, 0) % tiles_per_seq == 0)
    def _():
        hcarry_ref[...] = jnp.zeros_like(hcarry_ref)

    neg_c_softplus = -LRU_C * jax.nn.softplus(-ap_ref[...])
    first_row = lax.broadcasted_iota(jnp.int32, (SUBLANES, LRU_BLOCK), 0) == 0

    h_carry = [None] * LRU_BLOCKS

    def lru_tail(n, rc):
        cols = slice(n * LRU_BLOCK, (n + 1) * LRU_BLOCK)
        rs = slice(rc * TAIL_ROWS, (rc + 1) * TAIL_ROWS)
        r = jax.nn.sigmoid(pre_r_sc[rs, cols] + ba_ref[:, cols])
        gi = jax.nn.sigmoid(pre_i_sc[rs, cols] + bi_ref[:, cols])
        a_all = jnp.exp(neg_c_softplus[:, cols] * r)
        y = 1.0 - a_all * a_all
        u_all = y * lax.rsqrt(jnp.maximum(y, TINY)) * gi * xc_sc[rs, cols]
        h_prev = hcarry_ref[:, cols] if rc == 0 else h_carry[n]
        for j in range(TAIL_ROWS // SUBLANES):
            grp = slice(j * SUBLANES, (j + 1) * SUBLANES)
            a = a_all[grp, :]
            u = u_all[grp, :] + a * h_prev
            a = jnp.where(first_row, 0.0, a)
            for d in (1, 2):
                u = u + a * pltpu.roll(u, d, axis=0)
                a = a * pltpu.roll(a, d, axis=0)
            hj = u + a * pltpu.roll(u, 4, axis=0)
            out_rows = slice(rc * TAIL_ROWS + j * SUBLANES, rc * TAIL_ROWS + (j + 1) * SUBLANES)
            hg_ref[out_rows, cols] = (hj * gg_sc[out_rows, cols]).astype(BF16)
            h_prev = jnp.where(first_row, pltpu.roll(hj, 1, axis=0), 0.0)
        h_carry[n] = h_prev
        if (rc + 1) * TAIL_ROWS == rows:
            hcarry_ref[:, cols] = h_prev

    h_sc[0:rows, :] = _rms_norm(x_ref[...], g_ref[...]).astype(BF16)
    cos = cos_ref[...]
    sin = sin_ref[...]

    fence_row = rows + lax.shift_right_arithmetic(t, 31)

    def fence(hp):
        mark = jnp.concatenate([hp[:, :LANES], hp[:, LANES:2 * LANES]], axis=0).astype(BF16)
        h_sc[pl.ds(pl.multiple_of(fence_row, 2 * SUBLANES), 2 * SUBLANES), 0:LANES] = mark

    def mm(col):
        return jnp.dot(h_sc[0:rows, :], w_ref[:, col:col + MM_COLS], preferred_element_type=F32)

    cos_k = cos * RET_QK_DIM ** -0.5
    sin_k = sin * RET_QK_DIM ** -0.5

    def rotary(acc, out_ref, lo, c, s):
        half = RET_QK_DIM // 2
        t1 = acc[:, :half]
        t2 = acc[:, half:]
        out_ref[:, lo:lo + half] = (t1 * c - t2 * s).astype(BF16)
        out_ref[:, lo + half:lo + RET_QK_DIM] = (t1 * s + t2 * c).astype(BF16)

    def section(kind, i):
        lo = i * MM_COLS
        if kind == "q":
            rotary(mm(COL_Q + lo), q_ref, lo, cos, sin)
        elif kind == "k":
            rotary(mm(COL_K + lo), k_ref, lo, cos_k, sin_k)
        elif kind == "v":
            v_ref[:, lo:lo + MM_COLS] = mm(COL_V + lo).astype(BF16)
        elif kind == "g":
            sg_ref[:, lo:lo + MM_COLS] = jax.nn.silu(mm(COL_G + lo)).astype(BF16)
        elif kind == "mret":
            sr_ref[:, lo:lo + MM_COLS] = jax.nn.sigmoid(mm(COL_MRET + lo)).astype(BF16)
        else:
            sn_ref[:, lo:lo + MM_COLS] = jax.nn.sigmoid(mm(COL_MRNN + lo)).astype(BF16)

    def lru_front_a(n):
        cols = slice(n * LRU_BLOCK, (n + 1) * LRU_BLOCK)
        xr = mm(COL_XR + n * LRU_BLOCK)
        cat = jnp.concatenate([xcarry_ref[:, cols], xr], axis=0)
        xc = cb_ref[:, cols] + cw_ref[LRU_CONV - 1:LRU_CONV, cols] * xr
        for j in range(1, LRU_CONV):
            xc = xc + cw_ref[LRU_CONV - 1 - j:LRU_CONV - j, cols] * _shift_rows(cat, j, rows)
        xcarry_ref[:, cols] = xr[rows - SUBLANES:, :]
        xc_sc[:, cols] = xc
        return xc.astype(BF16)

    def lru_front_b(n, xcb):
        cols = slice(n * LRU_BLOCK, (n + 1) * LRU_BLOCK)
        gg_sc[:, cols] = _gelu_tanh(mm(COL_GR + n * LRU_BLOCK))
        pre_r_sc[:, cols] = jnp.dot(xcb, wa_ref[n], preferred_element_type=F32)
        pre_i_sc[:, cols] = jnp.dot(xcb, wi_ref[n], preferred_element_type=F32)

    tail_chunks = rows // TAIL_ROWS
    others = ([("v", i) for i in range(RET_V_WIDTH // MM_COLS)]
              + [("q", i) for i in range(RET_HEADS)] + [("k", i) for i in range(RET_HEADS)]
              + [("g", i) for i in range(RET_V_WIDTH // MM_COLS)]
              + [("mret", i) for i in range(D_MODEL // MM_COLS)]
              + [("mrnn", i) for i in range(D_MODEL // MM_COLS)])
    others.reverse()
    pending = []
    for n in range(LRU_BLOCKS):
        pieces = PIECES_PER_BLOCK if n == 0 else PIECES_PER_BLOCK - 2
        issued = 0
        for rc in range(tail_chunks):
            lru_tail(n, rc)
            if rc % FENCE_EVERY == FENCE_EVERY - 1:
                pending.append(h_carry[n])
                if len(pending) > FENCE_LAG:
                    fence(pending.pop(0))
            due =((rc + 1) * pieces) // tail_chunks
            while issued < due:
                section(*others.pop())
                issued += 1
        xcb = lru_front_a(n)
        section(*others.pop())
        lru_front_b(n, xcb)
    while others:
        section(*others.pop())


def _inproj(x2d, consts, batch, seq):
    m = x2d.shape[0]
    tm = INPROJ_ROWS
    tiles = seq // tm
    num_tiles = m // tm

    def cur(width):
        return pl.BlockSpec((tm, width), lambda t: (jnp.minimum(t, num_tiles - 1), 0))

    def table():
        return pl.BlockSpec((tm, LANES), lambda t: (jnp.minimum(t, num_tiles - 1) % tiles, 0))

    prev = pl.BlockSpec((tm, LRU_WIDTH), lambda t: (jnp.maximum(t - 1, 0), 0))

    g, w_in, cos, sin = consts[:4]
    lru_consts = consts[4:]
    out_widths = (RET_QK_WIDTH, RET_QK_WIDTH, RET_V_WIDTH, RET_V_WIDTH,
                  LRU_WIDTH, D_MODEL, D_MODEL)
    out_specs = [cur(w) for w in out_widths]
    out_specs[4] = prev
    staged = pltpu.VMEM((tm, LRU_WIDTH), F32)
    return pl.pallas_call(
        functools.partial(_inproj_kernel, tiles_per_seq=tiles, num_tiles=num_tiles),
        grid=(num_tiles + 1,),
        in_specs=[
            cur(D_MODEL),
            _const_spec(g.shape),
            _const_spec(w_in.shape),
            table(),
            table(),
        ] + [_const_spec(c.shape) for c in lru_consts],
        out_specs=out_specs,
        out_shape=[jax.ShapeDtypeStruct((m, w), BF16) for w in out_widths],
        scratch_shapes=[
            pltpu.VMEM((SUBLANES, LRU_WIDTH), F32),
            pltpu.VMEM((SUBLANES, LRU_WIDTH), F32),
            staged, staged, staged, staged,
            pltpu.VMEM((tm + 2 * SUBLANES, D_MODEL), BF16),
        ],
        compiler_params=pltpu.CompilerParams(
            dimension_semantics=("arbitrary",), vmem_limit_bytes=V7X_VMEM_LIMIT),
        name="inproj",
    )(x2d, g, w_in, cos, sin, *lru_consts)


def _mixer_kernel(q_ref, k_ref, v_ref, sg_ref, hg_ref, sr_ref, sn_ref, x_ref,
                  dch_ref, din_ref, dq_ref, dk_ref, gn_ref, wro_ref, wrn_ref, wout_ref,
                  o_ref,
                  state_ref, go_ref):
    rows = MIXER_ROWS

    @pl.when(pl.program_id(1) == 0)
    def _():
        state_ref[...] = jnp.zeros_like(state_ref)

    y_rnn = jnp.dot(hg_ref[...], wrn_ref[...], preferred_element_type=F32)

    for c in range(rows // RET_CHUNK):
        r0 = c * RET_CHUNK
        for hd in range(RET_HEADS):
            qk = slice(hd * RET_QK_DIM, (hd + 1) * RET_QK_DIM)
            vv = slice(hd * RET_V_DIM, (hd + 1) * RET_V_DIM)
            qh = q_ref[r0:r0 + RET_CHUNK, qk]
            kh = k_ref[r0:r0 + RET_CHUNK, qk]
            vh = v_ref[r0:r0 + RET_CHUNK, vv]
            scores = lax.dot_general(qh, kh, (((1,), (1,)), ((), ())),
                                     preferred_element_type=F32)
            scores = (scores * din_ref[hd]).astype(BF16)
            st = state_ref[hd]
            o = (jnp.dot(scores, vh, preferred_element_type=F32)
                 + jnp.dot(qh, st.astype(BF16), preferred_element_type=F32) * dq_ref[hd])
            kd = (kh.astype(F32) * dk_ref[hd]).astype(BF16)
            state_ref[hd] = st * dch_ref[hd] + lax.dot_general(
                kd, vh, (((0,), (0,)), ((), ())), preferred_element_type=F32)
            mu = jnp.mean(o, axis=-1, keepdims=True)
            dev = o - mu
            var = jnp.mean(dev * dev, axis=-1, keepdims=True)
            y = dev * lax.rsqrt(var + GN_EPS) * gn_ref[:, vv]
            go_ref[r0:r0 + RET_CHUNK, vv] = (
                y * sg_ref[r0:r0 + RET_CHUNK, vv].astype(F32)).astype(BF16)
    y_ret = jnp.dot(go_ref[...], wro_ref[...], preferred_element_type=F32)

    mixed = (sr_ref[...].astype(F32) * y_ret + sn_ref[...].astype(F32) * y_rnn).astype(BF16)
    o_ref[...] = x_ref[...] + jnp.dot(mixed, wout_ref[...], preferred_element_type=F32)


def _mixer(proj, x2d, consts, batch, seq):
    q, k, v, sg, hg, sr, sn = proj
    m = x2d.shape[0]
    t = MIXER_ROWS
    tiles = seq // t

    def rows(width):
        return pl.BlockSpec((t, width), lambda b, s: (b * tiles + s, 0))

    const_specs = [pl.BlockSpec(memory_space=pltpu.SMEM)]
    const_specs += [_const_spec(c.shape) for c in consts[1:]]
    return pl.pallas_call(
        _mixer_kernel,
        grid=(batch, tiles),
        in_specs=[rows(RET_QK_WIDTH), rows(RET_QK_WIDTH), rows(RET_V_WIDTH), rows(RET_V_WIDTH),
                  rows(LRU_WIDTH), rows(D_MODEL), rows(D_MODEL), rows(D_MODEL)]
                 + const_specs,
        out_specs=rows(D_MODEL),
        out_shape=jax.ShapeDtypeStruct((m, D_MODEL), F32),
        scratch_shapes=[
            pltpu.VMEM((RET_HEADS, RET_QK_DIM, RET_V_DIM), F32),
            pltpu.VMEM((t, RET_V_WIDTH), BF16),
        ],
        compiler_params=pltpu.CompilerParams(
            dimension_semantics=("arbitrary", "arbitrary"), vmem_limit_bytes=V7X_VMEM_LIMIT),
        name="mixer",
    )(q, k, v, sg, hg, sr, sn, x2d, *consts)


def _ffn_kernel(x_ref, g_ref, wup_ref, cw_ref, cb_ref, wdn_ref, fg_ref, o_ref, carry_ref):
    rows = FFN_ROWS

    @pl.when(pl.program_id(1) == 0)
    def _():
        carry_ref[...] = jnp.zeros_like(carry_ref)

    x = x_ref[...]
    h = _rms_norm(x, g_ref[...]).astype(BF16)

    def conv_up(col, width):
        cols = slice(col, col + width)
        up = jnp.dot(h, wup_ref[:, cols], preferred_element_type=F32)
        cat = jnp.concatenate([carry_ref[:, cols], up], axis=0)
        y = cb_ref[:, cols] + cw_ref[FFN_CONV - 1:FFN_CONV, cols] * up
        for j in range(1, FFN_CONV):
            y = y + cw_ref[FFN_CONV - 1 - j:FFN_CONV - j, cols] * _shift_rows(cat, j, rows)
        carry_ref[:, cols] = up[rows - SUBLANES:, :]
        return y

    acc = x
    nxt = (conv_up(FFN_SPLITS[0][0], FFN_SPLITS[0][1]),
           conv_up(FFN_HIDDEN + FFN_SPLITS[0][0], FFN_SPLITS[0][1]))
    for c, (col, width) in enumerate(FFN_SPLITS):
        gate, val = nxt
        if c + 1 < len(FFN_SPLITS):
            ncol, nwidth = FFN_SPLITS[c + 1]
            nxt = (conv_up(ncol, nwidth), conv_up(FFN_HIDDEN + ncol, nwidth))
        act = (jax.nn.silu(gate) * val).astype(BF16)
        acc = acc + jnp.dot(act, wdn_ref[col:col + width, :], preferred_element_type=F32)
    o_ref[...] = _rms_norm(acc, fg_ref[...])


def _ffn(x2d, consts, batch, seq):
    m = x2d.shape[0]
    t = FFN_ROWS
    tiles = seq // t
    rows = pl.BlockSpec((t, D_MODEL), lambda b, s: (b * tiles + s, 0))
    return pl.pallas_call(
        _ffn_kernel,
        grid=(batch, tiles),
        in_specs=[rows] + [_const_spec(c.shape) for c in consts],
        out_specs=rows,
        out_shape=jax.ShapeDtypeStruct((m, D_MODEL), F32),
        scratch_shapes=[pltpu.VMEM((SUBLANES, 2 * FFN_HIDDEN), F32)],
        compiler_params=pltpu.CompilerParams(
            dimension_semantics=("arbitrary", "arbitrary"), vmem_limit_bytes=V7X_VMEM_LIMIT),
        name="convffn",
    )(x2d, *consts)


def _retention_decays():
    heads = jnp.arange(RET_HEADS, dtype=F32)
    log_g = jnp.log1p(-jnp.exp2(-5.0 - heads))
    pos = jnp.arange(RET_CHUNK, dtype=F32)
    rel = pos[:, None] - pos[None, :]
    d_in = jnp.where(rel >= 0, jnp.exp(log_g[:, None, None] * jnp.maximum(rel, 0.0)), 0.0)
    d_q = jnp.exp(log_g[:, None] * (pos + 1.0))
    d_k = jnp.exp(log_g[:, None] * (RET_CHUNK - 1.0 - pos))
    d_chunk = jnp.exp(log_g * RET_CHUNK)
    d_q = jnp.broadcast_to(d_q[:, :, None], (RET_HEADS, RET_CHUNK, RET_V_DIM))
    d_k = jnp.broadcast_to(d_k[:, :, None], (RET_HEADS, RET_CHUNK, RET_QK_DIM))
    return d_chunk, d_in, d_q, d_k


def kernel(x, in_norm_g, w_in, ret_gn_g, w_ret_o, lru_conv_w, lru_conv_b, lru_w_a, lru_b_a,
           lru_w_i, lru_b_i, lru_a_param, w_rnn_o, w_out, ffn_norm_g, w_up, ffn_conv_w,
           ffn_conv_b, w_down, final_norm_g):
    batch, seq, d = x.shape
    depth = w_in.shape[0]
    assert d == D_MODEL and seq % INPROJ_ROWS == 0 and seq % MIXER_ROWS == 0 and seq % FFN_ROWS == 0
    assert depth == 1, "the final norm is fused into the (single) layer's FFN kernel"

    pos = jnp.arange(seq, dtype=F32)
    inv_freq = ROPE_BASE ** (-jnp.arange(0, RET_QK_DIM, 2, dtype=F32) / RET_QK_DIM)
    ang = pos[:, None] * inv_freq[None, :]
    cos = jnp.cos(ang)
    sin = jnp.sin(ang)
    d_chunk, d_in, d_q, d_k = _retention_decays()

    x2d = x.reshape(batch * seq, d)
    for l in range(depth):
        inproj_consts = (
            in_norm_g[l][None, :], w_in[l].astype(BF16), cos, sin,
            lru_conv_w[l], lru_conv_b[l][None, :],
            lru_w_a[l].astype(BF16), lru_b_a[l].reshape(1, LRU_WIDTH),
            lru_w_i[l].astype(BF16), lru_b_i[l].reshape(1, LRU_WIDTH),
            lru_a_param[l][None, :],
        )
        proj = _inproj(x2d, inproj_consts, batch, seq)
        mixer_consts = (
            d_chunk, d_in, d_q, d_k,
            ret_gn_g[l][None, :], w_ret_o[l].astype(BF16),
            w_rnn_o[l].astype(BF16), w_out[l].astype(BF16),
        )
        x2d = _mixer(proj, x2d, mixer_consts, batch, seq)
        ffn_consts = (
            ffn_norm_g[l][None, :], w_up[l].astype(BF16), ffn_conv_w[l],
            ffn_conv_b[l][None, :], w_down[l].astype(BF16),
            final_norm_g[None, :],
        )
        x2d = _ffn(x2d, ffn_consts, batch, seq)
    return x2d.reshape(batch, seq, d)
```

```python
import functools
import math

import jax
import jax.numpy as jnp
from jax import lax
from jax.experimental import pallas as pl
from jax.experimental.pallas import tpu as pltpu

F32 = jnp.float32
BF16 = jnp.bfloat16

D_MODEL = 1024
RET_HEADS = 4
RET_QK_DIM = 256
RET_V_DIM = 512
RET_QK_WIDTH = RET_HEADS * RET_QK_DIM
RET_V_WIDTH = RET_HEADS * RET_V_DIM
RET_CHUNK = 256
ROPE_BASE = 10000.0
LRU_WIDTH = 1024
LRU_BLOCK = 256
LRU_BLOCKS = 4
LRU_CONV = 4
LRU_C = 8.0
FFN_HIDDEN = 2816
FFN_CONV = 3
NORM_EPS = 1e-6
GN_EPS = 1e-5
GELU_A = 2.0 * (2.0 / math.pi) ** 0.5
GELU_B = 0.044715 * GELU_A
TINY = 1e-30

SUBLANES = 8
LANES = 128
V7X_VMEM_LIMIT = 60 * 1024 * 1024

COL_Q = 0
COL_K = COL_Q + RET_QK_WIDTH
COL_V = COL_K + RET_QK_WIDTH
COL_G = COL_V + RET_V_WIDTH
COL_XR = COL_G + RET_V_WIDTH
COL_GR = COL_XR + LRU_WIDTH
COL_MRET = COL_GR + LRU_WIDTH
COL_MRNN = COL_MRET + D_MODEL
IN_WIDTH = COL_MRNN + D_MODEL

INPROJ_ROWS = 512
MM_COLS = 256
TAIL_ROWS = 128
PIECES_PER_BLOCK = 8
FENCE_LAG = 1
MIXER_ROWS = 512
FFN_ROWS = 512
FFN_SPLITS = ((0, 768), (768, 768), (1536, 768), (2304, 512))


def _const_spec(shape):
    nd = len(shape)
    return pl.BlockSpec(shape, lambda *_: (0,) * nd, pipeline_mode=pl.Buffered(1))


def _rms_norm(x, g):
    ms = jnp.mean(x * x, axis=-1, keepdims=True)
    return x * lax.rsqrt(ms + NORM_EPS) * g


def _gelu_tanh(x):
    two_z = x * (GELU_A + GELU_B * (x * x))
    return x / (1.0 + jnp.exp(-two_z))


def _shift_rows(cat, j, rows):
    return pltpu.roll(cat, j, axis=0)[SUBLANES:SUBLANES + rows, :]


def _inproj_kernel(x_ref, g_ref, w_ref, cos_ref, sin_ref,
                   cw_ref, cb_ref, wa_ref, ba_ref, wi_ref, bi_ref, ap_ref,
                   q_ref, k_ref, v_ref, sg_ref, hg_ref, sr_ref, sn_ref,
                   xcarry_ref, hcarry_ref, pre_r_sc, pre_i_sc, xc_sc, gg_sc, h_sc,
                   *, tiles_per_seq, num_tiles):
    rows = INPROJ_ROWS
    t = pl.program_id(0)

    @pl.when(t == 0)
    def _():
        pre_r_sc[...] = jnp.zeros_like(pre_r_sc)
        pre_i_sc[...] = jnp.zeros_like(pre_i_sc)
        xc_sc[...] = jnp.zeros_like(xc_sc)
        gg_sc[...] = jnp.zeros_like(gg_sc)

    @pl.when(jnp.minimum(t, num_tiles - 1) % tiles_per_seq == 0)
    def _():
        xcarry_ref[...] = jnp.zeros_like(xcarry_ref)

    @pl.when(jnp.maximum(t - 1, 0) % tiles_per_seq == 0)
    def _():
        hcarry_ref[...] = jnp.zeros_like(hcarry_ref)

    neg_c_softplus = -LRU_C * jax.nn.softplus(-ap_ref[...])
    first_row = lax.broadcasted_iota(jnp.int32, (SUBLANES, LRU_BLOCK), 0) == 0

    h_carry = [None] * LRU_BLOCKS

    def lru_tail(n, rc):
        cols = slice(n * LRU_BLOCK, (n + 1) * LRU_BLOCK)
        rs = slice(rc * TAIL_ROWS, (rc + 1) * TAIL_ROWS)
        r = jax.nn.sigmoid(pre_r_sc[rs, cols] + ba_ref[:, cols])
        gi = jax.nn.sigmoid(pre_i_sc[rs, cols] + bi_ref[:, cols])
        a_all = jnp.exp(neg_c_softplus[:, cols] * r)
        y = 1.0 - a_all * a_all
        u_all = y * lax.rsqrt(jnp.maximum(y, TINY)) * gi * xc_sc[rs, cols]
        h_prev = hcarry_ref[:, cols] if rc == 0 else h_carry[n]
        for j in range(TAIL_ROWS // SUBLANES):
            grp = slice(j * SUBLANES, (j + 1) * SUBLANES)
            a = a_all[grp, :]
            u = u_all[grp, :] + a * h_prev
            a = jnp.where(first_row, 0.0, a)
            for d in (1, 2):
                u = u + a * pltpu.roll(u, d, axis=0)
                a = a * pltpu.roll(a, d, axis=0)
            hj = u + a * pltpu.roll(u, 4, axis=0)
            out_rows = slice(rc * TAIL_ROWS + j * SUBLANES, rc * TAIL_ROWS + (j + 1) * SUBLANES)
            hg_ref[out_rows, cols] = (hj * gg_sc[out_rows, cols]).astype(BF16)
            h_prev = jnp.where(first_row, pltpu.roll(hj, 1, axis=0), 0.0)
        h_carry[n] = h_prev
        if (rc + 1) * TAIL_ROWS == rows:
            hcarry_ref[:, cols] = h_prev

    h_sc[0:rows, :] = _rms_norm(x_ref[...], g_ref[...]).astype(BF16)
    cos = cos_ref[...]
    sin = sin_ref[...]

    fence_row = rows + lax.shift_right_arithmetic(t, 31)

    def fence(hp):
        mark = jnp.concatenate([hp[:, :LANES], hp[:, LANES:2 * LANES]], axis=0).astype(BF16)
        h_sc[pl.ds(pl.multiple_of(fence_row, 2 * SUBLANES), 2 * SUBLANES), 0:LANES] = mark

    def mm(col):
        return jnp.dot(h_sc[0:rows, :], w_ref[:, col:col + MM_COLS], preferred_element_type=F32)

    cos_k = cos * RET_QK_DIM ** -0.5
    sin_k = sin * RET_QK_DIM ** -0.5

    def rotary(acc, out_ref, lo, c, s):
        half = RET_QK_DIM // 2
        t1 = acc[:, :half]
        t2 = acc[:, half:]
        out_ref[:, lo:lo + half] = (t1 * c - t2 * s).astype(BF16)
        out_ref[:, lo + half:lo + RET_QK_DIM] = (t1 * s + t2 * c).astype(BF16)

    def section(kind, i):
        lo = i * MM_COLS
        if kind == "q":
            rotary(mm(COL_Q + lo), q_ref, lo, cos, sin)
        elif kind == "k":
            rotary(mm(COL_K + lo), k_ref, lo, cos_k, sin_k)
        elif kind == "v":
            v_ref[:, lo:lo + MM_COLS] = mm(COL_V + lo).astype(BF16)
        elif kind == "g":
            sg_ref[:, lo:lo + MM_COLS] = jax.nn.silu(mm(COL_G + lo)).astype(BF16)
        elif kind == "mret":
            sr_ref[:, lo:lo + MM_COLS] = jax.nn.sigmoid(mm(COL_MRET + lo)).astype(BF16)
        else:
            sn_ref[:, lo:lo + MM_COLS] = jax.nn.sigmoid(mm(COL_MRNN + lo)).astype(BF16)

    def lru_front_a(n):
        cols = slice(n * LRU_BLOCK, (n + 1) * LRU_BLOCK)
        xr = mm(COL_XR + n * LRU_BLOCK)
        cat = jnp.concatenate([xcarry_ref[:, cols], xr], axis=0)
        xc = cb_ref[:, cols] + cw_ref[LRU_CONV - 1:LRU_CONV, cols] * xr
        for j in range(1, LRU_CONV):
            xc = xc + cw_ref[LRU_CONV - 1 - j:LRU_CONV - j, cols] * _shift_rows(cat, j, rows)
        xcarry_ref[:, cols] = xr[rows - SUBLANES:, :]
        xc_sc[:, cols] = xc
        return xc.astype(BF16)

    def lru_front_b(n, xcb):
        cols = slice(n * LRU_BLOCK, (n + 1) * LRU_BLOCK)
        gg_sc[:, cols] = _gelu_tanh(mm(COL_GR + n * LRU_BLOCK))
        pre_r_sc[:, cols] = jnp.dot(xcb, wa_ref[n], preferred_element_type=F32)
        pre_i_sc[:, cols] = jnp.dot(xcb, wi_ref[n], preferred_element_type=F32)

    tail_chunks = rows // TAIL_ROWS
    others = ([("v", i) for i in range(RET_V_WIDTH // MM_COLS)]
              + [("q", i) for i in range(RET_HEADS)] + [("k", i) for i in range(RET_HEADS)]
              + [("g", i) for i in range(RET_V_WIDTH // MM_COLS)]
              + [("mret", i) for i in range(D_MODEL // MM_COLS)]
              + [("mrnn", i) for i in range(D_MODEL // MM_COLS)])
    others.reverse()
    pending = []
    for n in range(LRU_BLOCKS):
        pieces = PIECES_PER_BLOCK if n == 0 else PIECES_PER_BLOCK - 2
        issued = 0
        for rc in range(tail_chunks):
            lru_tail(n, rc)
            pending.append(h_carry[n])
            if len(pending) > FENCE_LAG:
                fence(pending.pop(0))
            due = ((rc + 1) * pieces) // tail_chunks
            while issued < due:
                section(*others.pop())
                issued += 1
        xcb = lru_front_a(n)
        section(*others.pop())
        lru_front_b(n, xcb)
    while others:
        section(*others.pop())


def _inproj(x2d, consts, batch, seq):
    m = x2d.shape[0]
    tm = INPROJ_ROWS
    tiles = seq // tm
    num_tiles = m // tm

    def cur(width):
        return pl.BlockSpec((tm, width), lambda t: (jnp.minimum(t, num_tiles - 1), 0))

    def table():
        return pl.BlockSpec((tm, LANES), lambda t: (jnp.minimum(t, num_tiles - 1) % tiles, 0))

    prev = pl.BlockSpec((tm, LRU_WIDTH), lambda t: (jnp.maximum(t - 1, 0), 0))

    g, w_in, cos, sin = consts[:4]
    lru_consts = consts[4:]
    out_widths = (RET_QK_WIDTH, RET_QK_WIDTH, RET_V_WIDTH, RET_V_WIDTH,
                  LRU_WIDTH, D_MODEL, D_MODEL)
    out_specs = [cur(w) for w in out_widths]
    out_specs[4] = prev
    staged = pltpu.VMEM((tm, LRU_WIDTH), F32)
    return pl.pallas_call(
        functools.partial(_inproj_kernel, tiles_per_seq=tiles, num_tiles=num_tiles),
        grid=(num_tiles + 1,),
        in_specs=[
            cur(D_MODEL),
            _const_spec(g.shape),
            _const_spec(w_in.shape),
            table(),
            table(),
        ] + [_const_spec(c.shape) for c in lru_consts],
        out_specs=out_specs,
        out_shape=[jax.ShapeDtypeStruct((m, w), BF16) for w in out_widths],
        scratch_shapes=[
            pltpu.VMEM((SUBLANES, LRU_WIDTH), F32),
            pltpu.VMEM((SUBLANES, LRU_WIDTH), F32),
            staged, staged, staged, staged,
            pltpu.VMEM((tm + 2 * SUBLANES, D_MODEL), BF16),
        ],
        compiler_params=pltpu.CompilerParams(
            dimension_semantics=("arbitrary",), vmem_limit_bytes=V7X_VMEM_LIMIT),
        name="inproj",
    )(x2d, g, w_in, cos, sin, *lru_consts)


def _mixer_kernel(q_ref, k_ref, v_ref, sg_ref, hg_ref, sr_ref, sn_ref, x_ref,
                  dch_ref, din_ref, dq_ref, dk_ref, gn_ref, wro_ref, wrn_ref, wout_ref,
                  o_ref,
                  state_ref, go_ref):
    rows = MIXER_ROWS

    @pl.when(pl.program_id(1) == 0)
    def _():
        state_ref[...] = jnp.zeros_like(state_ref)

    y_rnn = jnp.dot(hg_ref[...], wrn_ref[...], preferred_element_type=F32)

    for c in range(rows // RET_CHUNK):
        r0 = c * RET_CHUNK
        for hd in range(RET_HEADS):
            qk = slice(hd * RET_QK_DIM, (hd + 1) * RET_QK_DIM)
            vv = slice(hd * RET_V_DIM, (hd + 1) * RET_V_DIM)
            qh = q_ref[r0:r0 + RET_CHUNK, qk]
            kh = k_ref[r0:r0 + RET_CHUNK, qk]
            vh = v_ref[r0:r0 + RET_CHUNK, vv]
            scores = lax.dot_general(qh, kh, (((1,), (1,)), ((), ())),
                                     preferred_element_type=F32)
            scores = (scores * din_ref[hd]).astype(BF16)
            st = state_ref[hd]
            o = (jnp.dot(scores, vh, preferred_element_type=F32)
                 + jnp.dot(qh, st.astype(BF16), preferred_element_type=F32) * dq_ref[hd])
            kd = (kh.astype(F32) * dk_ref[hd]).astype(BF16)
            state_ref[hd] = st * dch_ref[hd] + lax.dot_general(
                kd, vh, (((0,), (0,)), ((), ())), preferred_element_type=F32)
            mu = jnp.mean(o, axis=-1, keepdims=True)
            dev = o - mu
            var = jnp.mean(dev * dev, axis=-1, keepdims=True)
            y = dev * lax.rsqrt(var + GN_EPS) * gn_ref[:, vv]
            go_ref[r0:r0 + RET_CHUNK, vv] = (
                y * sg_ref[r0:r0 + RET_CHUNK, vv].astype(F32)).astype(BF16)
    y_ret = jnp.dot(go_ref[...], wro_ref[...], preferred_element_type=F32)

    mixed = (sr_ref[...].astype(F32) * y_ret + sn_ref[...].astype(F32) * y_rnn).astype(BF16)
    o_ref[...] = x_ref[...] + jnp.dot(mixed, wout_ref[...], preferred_element_type=F32)


def _mixer(proj, x2d, consts, batch, seq):
    q, k, v, sg, hg, sr, sn = proj
    m = x2d.shape[0]
    t = MIXER_ROWS
    tiles = seq // t

    def rows(width):
        return pl.BlockSpec((t, width), lambda b, s: (b * tiles + s, 0))

    const_specs = [pl.BlockSpec(memory_space=pltpu.SMEM)]
    const_specs += [_const_spec(c.shape) for c in consts[1:]]
    return pl.pallas_call(
        _mixer_kernel,
        grid=(batch, tiles),
        in_specs=[rows(RET_QK_WIDTH), rows(RET_QK_WIDTH), rows(RET_V_WIDTH), rows(RET_V_WIDTH),
                  rows(LRU_WIDTH), rows(D_MODEL), rows(D_MODEL), rows(D_MODEL)]
                 + const_specs,
        out_specs=rows(D_MODEL),
        out_shape=jax.ShapeDtypeStruct((m, D_MODEL), F32),
        scratch_shapes=[
            pltpu.VMEM((RET_HEADS, RET_QK_DIM, RET_V_DIM), F32),
            pltpu.VMEM((t, RET_V_WIDTH), BF16),
        ],
        compiler_params=pltpu.CompilerParams(
            dimension_semantics=("arbitrary", "arbitrary"), vmem_limit_bytes=V7X_VMEM_LIMIT),
        name="mixer",
    )(q, k, v, sg, hg, sr, sn, x2d, *consts)


def _ffn_kernel(x_ref, g_ref, wup_ref, cw_ref, cb_ref, wdn_ref, fg_ref, o_ref, carry_ref):
    rows = FFN_ROWS

    @pl.when(pl.program_id(1) == 0)
    def _():
        carry_ref[...] = jnp.zeros_like(carry_ref)

    x = x_ref[...]
    h = _rms_norm(x, g_ref[...]).astype(BF16)

    def conv_up(col, width):
        cols = slice(col, col + width)
        up = jnp.dot(h, wup_ref[:, cols], preferred_element_type=F32)
        cat = jnp.concatenate([carry_ref[:, cols], up], axis=0)
        y = cb_ref[:, cols] + cw_ref[FFN_CONV - 1:FFN_CONV, cols] * up
        for j in range(1, FFN_CONV):
            y = y + cw_ref[FFN_CONV - 1 - j:FFN_CONV - j, cols] * _shift_rows(cat, j, rows)
        carry_ref[:, cols] = up[rows - SUBLANES:, :]
        return y

    acc = x
    acts = []
    nxt = (conv_up(FFN_SPLITS[0][0], FFN_SPLITS[0][1]),
           conv_up(FFN_HIDDEN + FFN_SPLITS[0][0], FFN_SPLITS[0][1]))
    for c, (col, width) in enumerate(FFN_SPLITS):
        gate, val = nxt
        if c + 1 < len(FFN_SPLITS):
            ncol, nwidth = FFN_SPLITS[c + 1]
            nxt = (conv_up(ncol, nwidth), conv_up(FFN_HIDDEN + ncol, nwidth))
        acts.append((jax.nn.silu(gate) * val).astype(BF16))
    acc = acc + jnp.dot(jnp.concatenate(acts, axis=1), wdn_ref[...], preferred_element_type=F32)
    o_ref[...] = _rms_norm(acc, fg_ref[...])


def _ffn(x2d, consts, batch, seq):
    m = x2d.shape[0]
    t = FFN_ROWS
    tiles = seq // t
    rows = pl.BlockSpec((t, D_MODEL), lambda b, s: (b * tiles + s, 0))
    return pl.pallas_call(
        _ffn_kernel,
        grid=(batch, tiles),
        in_specs=[rows] + [_const_spec(c.shape) for c in consts],
        out_specs=rows,
        out_shape=jax.ShapeDtypeStruct((m, D_MODEL), F32),
        scratch_shapes=[pltpu.VMEM((SUBLANES, 2 * FFN_HIDDEN), F32)],
        compiler_params=pltpu.CompilerParams(
            dimension_semantics=("arbitrary", "arbitrary"), vmem_limit_bytes=V7X_VMEM_LIMIT),
        name="convffn",
    )(x2d, *consts)


def _retention_decays():
    heads = jnp.arange(RET_HEADS, dtype=F32)
    log_g = jnp.log1p(-jnp.exp2(-5.0 - heads))
    pos = jnp.arange(RET_CHUNK, dtype=F32)
    rel = pos[:, None] - pos[None, :]
    d_in = jnp.where(rel >= 0, jnp.exp(log_g[:, None, None] * jnp.maximum(rel, 0.0)), 0.0)
    d_q = jnp.exp(log_g[:, None] * (pos + 1.0))
    d_k = jnp.exp(log_g[:, None] * (RET_CHUNK - 1.0 - pos))
    d_chunk = jnp.exp(log_g * RET_CHUNK)
    d_q = jnp.broadcast_to(d_q[:, :, None], (RET_HEADS, RET_CHUNK, RET_V_DIM))
    d_k = jnp.broadcast_to(d_k[:, :, None], (RET_HEADS, RET_CHUNK, RET_QK_DIM))
    return d_chunk, d_in, d_q, d_k


def kernel(x, in_norm_g, w_in, ret_gn_g, w_ret_o, lru_conv_w, lru_conv_b, lru_w_a, lru_b_a,
           lru_w_i, lru_b_i, lru_a_param, w_rnn_o, w_out, ffn_norm_g, w_up, ffn_conv_w,
           ffn_conv_b, w_down, final_norm_g):
    batch, seq, d = x.shape
    depth = w_in.shape[0]
    assert d == D_MODEL and seq % INPROJ_ROWS == 0 and seq % MIXER_ROWS == 0 and seq % FFN_ROWS == 0
    assert depth == 1, "the final norm is fused into the (single) layer's FFN kernel"

    pos = jnp.arange(seq, dtype=F32)
    inv_freq = ROPE_BASE ** (-jnp.arange(0, RET_QK_DIM, 2, dtype=F32) / RET_QK_DIM)
    ang = pos[:, None] * inv_freq[None, :]
    cos = jnp.cos(ang)
    sin = jnp.sin(ang)
    d_chunk, d_in, d_q, d_k = _retention_decays()

    x2d = x.reshape(batch * seq, d)
    for l in range(depth):
        inproj_consts = (
            in_norm_g[l][None, :], w_in[l].astype(BF16), cos, sin,
            lru_conv_w[l], lru_conv_b[l][None, :],
            lru_w_a[l].astype(BF16), lru_b_a[l].reshape(1, LRU_WIDTH),
            lru_w_i[l].astype(BF16), lru_b_i[l].reshape(1, LRU_WIDTH),
            lru_a_param[l][None, :],
        )
        proj = _inproj(x2d, inproj_consts, batch, seq)
        mixer_consts = (
            d_chunk, d_in, d_q, d_k,
            ret_gn_g[l][None, :], w_ret_o[l].astype(BF16),
            w_rnn_o[l].astype(BF16), w_out[l].astype(BF16),
        )
        x2d = _mixer(proj, x2d, mixer_consts, batch, seq)
        ffn_consts = (
            ffn_norm_g[l][None, :], w_up[l].astype(BF16), ffn_conv_w[l],
            ffn_conv_b[l][None, :], w_down[l].astype(BF16),
            final_norm_g[None, :],
        )
        x2d = _ffn(x2d, ffn_consts, batch, seq)
    return x2d.reshape(batch, seq, d)
```

```python
import functools
import math

import jax
import jax.numpy as jnp
from jax import lax
from jax.experimental import pallas as pl
from jax.experimental.pallas import tpu as pltpu

F32 = jnp.float32
BF16 = jnp.bfloat16

D_MODEL = 1024
RET_HEADS = 4
RET_QK_DIM = 256
RET_V_DIM = 512
RET_QK_WIDTH = RET_HEADS * RET_QK_DIM
RET_V_WIDTH = RET_HEADS * RET_V_DIM
RET_CHUNK = 256
ROPE_BASE = 10000.0
LRU_WIDTH = 1024
LRU_BLOCK = 256
LRU_BLOCKS = 4
LRU_CONV = 4
LRU_C = 8.0
FFN_HIDDEN = 2816
FFN_CONV = 3
NORM_EPS = 1e-6
GN_EPS = 1e-5
GELU_A = 2.0 * (2.0 / math.pi) ** 0.5
GELU_B = 0.044715 * GELU_A
TINY = 1e-30

SUBLANES = 8
LANES = 128
V7X_VMEM_LIMIT = 60 * 1024 * 1024

COL_Q = 0
COL_K = COL_Q + RET_QK_WIDTH
COL_V = COL_K + RET_QK_WIDTH
COL_G = COL_V + RET_V_WIDTH
COL_XR = COL_G + RET_V_WIDTH
COL_GR = COL_XR + LRU_WIDTH
COL_MRET = COL_GR + LRU_WIDTH
COL_MRNN = COL_MRET + D_MODEL
IN_WIDTH = COL_MRNN + D_MODEL

INPROJ_ROWS = 512
MM_COLS = 256
TAIL_ROWS = 128
PIECES_PER_BLOCK = 8
FENCE_LAG = 1
MIXER_ROWS = 512
FFN_ROWS = 512
FFN_SPLITS = ((0, 768), (768, 768), (1536, 768), (2304, 512))


def _const_spec(shape):
    nd = len(shape)
    return pl.BlockSpec(shape, lambda *_: (0,) * nd, pipeline_mode=pl.Buffered(1))


def _rms_norm(x, g):
    ms = jnp.mean(x * x, axis=-1, keepdims=True)
    return x * lax.rsqrt(ms + NORM_EPS) * g


def _gelu_tanh(x):
    two_z = x * (GELU_A + GELU_B * (x * x))
    return x / (1.0 + jnp.exp(-two_z))


def _shift_rows(cat, j, rows):
    return pltpu.roll(cat, j, axis=0)[SUBLANES:SUBLANES + rows, :]


def _inproj_kernel(x_ref, g_ref, w_ref, cos_ref, sin_ref,
                   cw_ref, cb_ref, wa_ref, ba_ref, wi_ref, bi_ref, ap_ref,
                   q_ref, k_ref, v_ref, sg_ref, hg_ref, sr_ref, sn_ref,
                   xcarry_ref, hcarry_ref, pre_r_sc, pre_i_sc, xc_sc, gg_sc, h_sc,
                   *, tiles_per_seq, num_tiles):
    rows = INPROJ_ROWS
    t = pl.program_id(0)

    @pl.when(t == 0)
    def _():
        pre_r_sc[...] = jnp.zeros_like(pre_r_sc)
        pre_i_sc[...] = jnp.zeros_like(pre_i_sc)
        xc_sc[...] = jnp.zeros_like(xc_sc)
        gg_sc[...] = jnp.zeros_like(gg_sc)

    @pl.when(jnp.minimum(t, num_tiles - 1) % tiles_per_seq == 0)
    def _():
        xcarry_ref[...] = jnp.zeros_like(xcarry_ref)

    @pl.when(jnp.maximum(t - 1, 0) % tiles_per_seq == 0)
    def _():
        hcarry_ref[...] = jnp.zeros_like(hcarry_ref)

    neg_c_softplus = -LRU_C * jax.nn.softplus(-ap_ref[...])
    first_row = lax.broadcasted_iota(jnp.int32, (SUBLANES, LRU_BLOCK), 0) == 0

    h_carry = [None] * LRU_BLOCKS

    def lru_tail(n, rc):
        cols = slice(n * LRU_BLOCK, (n + 1) * LRU_BLOCK)
        rs = slice(rc * TAIL_ROWS, (rc + 1) * TAIL_ROWS)
        r = jax.nn.sigmoid(pre_r_sc[rs, cols] + ba_ref[:, cols])
        gi = jax.nn.sigmoid(pre_i_sc[rs, cols] + bi_ref[:, cols])
        a_all = jnp.exp(neg_c_softplus[:, cols] * r)
        y = 1.0 - a_all * a_all
        u_all = y * lax.rsqrt(jnp.maximum(y, TINY)) * gi * xc_sc[rs, cols]
        h_prev = hcarry_ref[:, cols] if rc == 0 else h_carry[n]
        for j in range(TAIL_ROWS // SUBLANES):
            grp = slice(j * SUBLANES, (j + 1) * SUBLANES)
            a = a_all[grp, :]
            u = u_all[grp, :] + a * h_prev
            a = jnp.where(first_row, 0.0, a)
            for d in (1, 2):
                u = u + a * pltpu.roll(u, d, axis=0)
                a = a * pltpu.roll(a, d, axis=0)
            hj = u + a * pltpu.roll(u, 4, axis=0)
            out_rows = slice(rc * TAIL_ROWS + j * SUBLANES, rc * TAIL_ROWS + (j + 1) * SUBLANES)
            hg_ref[out_rows, cols] = (hj * gg_sc[out_rows, cols]).astype(BF16)
            h_prev = jnp.where(first_row, pltpu.roll(hj, 1, axis=0), 0.0)
        h_carry[n] = h_prev
        if (rc + 1) * TAIL_ROWS == rows:
            hcarry_ref[:, cols] = h_prev

    h_sc[0:rows, :] = _rms_norm(x_ref[...], g_ref[...]).astype(BF16)
    cos = cos_ref[...]
    sin = sin_ref[...]

    fence_row = rows + lax.shift_right_arithmetic(t, 31)

    def fence(hp):
        mark = jnp.concatenate([hp[:, :LANES], hp[:, LANES:2 * LANES]], axis=0).astype(BF16)
        h_sc[pl.ds(pl.multiple_of(fence_row, 2 * SUBLANES), 2 * SUBLANES), 0:LANES] = mark

    def mm(col):
        return jnp.dot(h_sc[0:rows, :], w_ref[:, col:col + MM_COLS], preferred_element_type=F32)

    cos_k = cos * RET_QK_DIM ** -0.5
    sin_k = sin * RET_QK_DIM ** -0.5

    def rotary(acc, out_ref, lo, c, s):
        half = RET_QK_DIM // 2
        t1 = acc[:, :half]
        t2 = acc[:, half:]
        out_ref[:, lo:lo + half] = (t1 * c - t2 * s).astype(BF16)
        out_ref[:, lo + half:lo + RET_QK_DIM] = (t1 * s + t2 * c).astype(BF16)

    def section(kind, i):
        lo = i * MM_COLS
        if kind == "q":
            rotary(mm(COL_Q + lo), q_ref, lo, cos, sin)
        elif kind == "k":
            rotary(mm(COL_K + lo), k_ref, lo, cos_k, sin_k)
        elif kind == "v":
            v_ref[:, lo:lo + MM_COLS] = mm(COL_V + lo).astype(BF16)
        elif kind == "g":
            sg_ref[:, lo:lo + MM_COLS] = jax.nn.silu(mm(COL_G + lo)).astype(BF16)
        elif kind == "mret":
            sr_ref[:, lo:lo + MM_COLS] = jax.nn.sigmoid(mm(COL_MRET + lo)).astype(BF16)
        else:
            sn_ref[:, lo:lo + MM_COLS] = jax.nn.sigmoid(mm(COL_MRNN + lo)).astype(BF16)

    def lru_front_a(n):
        cols = slice(n * LRU_BLOCK, (n + 1) * LRU_BLOCK)
        xr = mm(COL_XR + n * LRU_BLOCK)
        cat = jnp.concatenate([xcarry_ref[:, cols], xr], axis=0)
        xc = cb_ref[:, cols] + cw_ref[LRU_CONV - 1:LRU_CONV, cols] * xr
        for j in range(1, LRU_CONV):
            xc = xc + cw_ref[LRU_CONV - 1 - j:LRU_CONV - j, cols] * _shift_rows(cat, j, rows)
        xcarry_ref[:, cols] = xr[rows - SUBLANES:, :]
        xc_sc[:, cols] = xc
        return xc.astype(BF16)

    def lru_front_b(n, xcb):
        cols = slice(n * LRU_BLOCK, (n + 1) * LRU_BLOCK)
        gg_sc[:, cols] = _gelu_tanh(mm(COL_GR + n * LRU_BLOCK))
        pre_r_sc[:, cols] = jnp.dot(xcb, wa_ref[n], preferred_element_type=F32)
        pre_i_sc[:, cols] = jnp.dot(xcb, wi_ref[n], preferred_element_type=F32)

    tail_chunks = rows // TAIL_ROWS
    others = ([("v", i) for i in range(RET_V_WIDTH // MM_COLS)]
              + [("q", i) for i in range(RET_HEADS)] + [("k", i) for i in range(RET_HEADS)]
              + [("g", i) for i in range(RET_V_WIDTH // MM_COLS)]
              + [("mret", i) for i in range(D_MODEL // MM_COLS)]
              + [("mrnn", i) for i in range(D_MODEL // MM_COLS)])
    others.reverse()
    pending = []
    for n in range(LRU_BLOCKS):
        pieces = PIECES_PER_BLOCK if n == 0 else PIECES_PER_BLOCK - 2
        issued = 0
        for rc in range(tail_chunks):
            lru_tail(n, rc)
            pending.append(h_carry[n])
            if len(pending) > FENCE_LAG:
                fence(pending.pop(0))
            due = ((rc + 1) * pieces) // tail_chunks
            while issued < due:
                section(*others.pop())
                issued += 1
        xcb = lru_front_a(n)
        section(*others.pop())
        lru_front_b(n, xcb)
    while others:
        section(*others.pop())


def _inproj(x2d, consts, batch, seq):
    m = x2d.shape[0]
    tm = INPROJ_ROWS
    tiles = seq // tm
    num_tiles = m // tm

    def cur(width):
        return pl.BlockSpec((tm, width), lambda t: (jnp.minimum(t, num_tiles - 1), 0))

    def table():
        return pl.BlockSpec((tm, LANES), lambda t: (jnp.minimum(t, num_tiles - 1) % tiles, 0))

    prev = pl.BlockSpec((tm, LRU_WIDTH), lambda t: (jnp.maximum(t - 1, 0), 0))

    g, w_in, cos, sin = consts[:4]
    lru_consts = consts[4:]
    out_widths = (RET_QK_WIDTH, RET_QK_WIDTH, RET_V_WIDTH, RET_V_WIDTH,
                  LRU_WIDTH, D_MODEL, D_MODEL)
    out_specs = [cur(w) for w in out_widths]
    out_specs[4] = prev
    staged = pltpu.VMEM((tm, LRU_WIDTH), F32)
    return pl.pallas_call(
        functools.partial(_inproj_kernel, tiles_per_seq=tiles, num_tiles=num_tiles),
        grid=(num_tiles + 1,),
        in_specs=[
            cur(D_MODEL),
            _const_spec(g.shape),
            _const_spec(w_in.shape),
            table(),
            table(),
        ] + [_const_spec(c.shape) for c in lru_consts],
        out_specs=out_specs,
        out_shape=[jax.ShapeDtypeStruct((m, w), BF16) for w in out_widths],
        scratch_shapes=[
            pltpu.VMEM((SUBLANES, LRU_WIDTH), F32),
            pltpu.VMEM((SUBLANES, LRU_WIDTH), F32),
            staged, staged, staged, staged,
            pltpu.VMEM((tm + 2 * SUBLANES, D_MODEL), BF16),
        ],
        compiler_params=pltpu.CompilerParams(
            dimension_semantics=("arbitrary",), vmem_limit_bytes=V7X_VMEM_LIMIT),
        name="inproj",
    )(x2d, g, w_in, cos, sin, *lru_consts)


def _mixer_kernel(q_ref, k_ref, v_ref, sg_ref, hg_ref, sr_ref, sn_ref, x_ref,
                  dch_ref, din_ref, dq_ref, dk_ref, gn_ref, wro_ref, wrn_ref, wout_ref,
                  o_ref,
                  state_ref, go_ref):
    rows = MIXER_ROWS

    @pl.when(pl.program_id(1) == 0)
    def _():
        state_ref[...] = jnp.zeros_like(state_ref)

    y_rnn = jnp.dot(hg_ref[...], wrn_ref[...], preferred_element_type=F32)

    for c in range(rows // RET_CHUNK):
        r0 = c * RET_CHUNK
        for hd in range(RET_HEADS):
            qk = slice(hd * RET_QK_DIM, (hd + 1) * RET_QK_DIM)
            vv = slice(hd * RET_V_DIM, (hd + 1) * RET_V_DIM)
            qh = q_ref[r0:r0 + RET_CHUNK, qk]
            kh = k_ref[r0:r0 + RET_CHUNK, qk]
            vh = v_ref[r0:r0 + RET_CHUNK, vv]
            scores = lax.dot_general(qh, kh, (((1,), (1,)), ((), ())),
                                     preferred_element_type=F32)
            scores = (scores * din_ref[hd]).astype(BF16)
            st = state_ref[hd]
            qd = (qh.astype(F32) * dq_ref[hd][:, :RET_QK_DIM]).astype(BF16)
            o = jnp.dot(jnp.concatenate([scores, qd], axis=1),
                        jnp.concatenate([vh, st.astype(BF16)], axis=0),
                        preferred_element_type=F32)
            kd = (kh.astype(F32) * dk_ref[hd]).astype(BF16)
            state_ref[hd] = st * dch_ref[hd] + lax.dot_general(
                kd, vh, (((0,), (0,)), ((), ())), preferred_element_type=F32)
            mu = jnp.mean(o, axis=-1, keepdims=True)
            dev = o - mu
            var = jnp.mean(dev * dev, axis=-1, keepdims=True)
            y = dev * lax.rsqrt(var + GN_EPS) * gn_ref[:, vv]
            go_ref[r0:r0 + RET_CHUNK, vv] = (
                y * sg_ref[r0:r0 + RET_CHUNK, vv].astype(F32)).astype(BF16)
    y_ret = jnp.dot(go_ref[...], wro_ref[...], preferred_element_type=F32)

    mixed = (sr_ref[...].astype(F32) * y_ret + sn_ref[...].astype(F32) * y_rnn).astype(BF16)
    o_ref[...] = x_ref[...] + jnp.dot(mixed, wout_ref[...], preferred_element_type=F32)


def _mixer(proj, x2d, consts, batch, seq):
    q, k, v, sg, hg, sr, sn = proj
    m = x2d.shape[0]
    t = MIXER_ROWS
    tiles = seq // t

    def rows(width):
        return pl.BlockSpec((t, width), lambda b, s: (b * tiles + s, 0))

    const_specs = [pl.BlockSpec(memory_space=pltpu.SMEM)]
    const_specs += [_const_spec(c.shape) for c in consts[1:]]
    return pl.pallas_call(
        _mixer_kernel,
        grid=(batch, tiles),
        in_specs=[rows(RET_QK_WIDTH), rows(RET_QK_WIDTH), rows(RET_V_WIDTH), rows(RET_V_WIDTH),
                  rows(LRU_WIDTH), rows(D_MODEL), rows(D_MODEL), rows(D_MODEL)]
                 + const_specs,
        out_specs=rows(D_MODEL),
        out_shape=jax.ShapeDtypeStruct((m, D_MODEL), F32),
        scratch_shapes=[
            pltpu.VMEM((RET_HEADS, RET_QK_DIM, RET_V_DIM), F32),
            pltpu.VMEM((t, RET_V_WIDTH), BF16),
        ],
        compiler_params=pltpu.CompilerParams(
            dimension_semantics=("arbitrary", "arbitrary"), vmem_limit_bytes=V7X_VMEM_LIMIT),
        name="mixer",
    )(q, k, v, sg, hg, sr, sn, x2d, *consts)


def _ffn_kernel(x_ref, g_ref, wup_ref, cw_ref, cb_ref, wdn_ref, fg_ref, o_ref, carry_ref):
    rows = FFN_ROWS

    @pl.when(pl.program_id(1) == 0)
    def _():
        carry_ref[...] = jnp.zeros_like(carry_ref)

    x = x_ref[...]
    h = _rms_norm(x, g_ref[...]).astype(BF16)

    def conv_up(col, width):
        cols = slice(col, col + width)
        up = jnp.dot(h, wup_ref[:, cols], preferred_element_type=F32)
        cat = jnp.concatenate([carry_ref[:, cols], up], axis=0)
        y = cb_ref[:, cols] + cw_ref[FFN_CONV - 1:FFN_CONV, cols] * up
        for j in range(1, FFN_CONV):
            y = y + cw_ref[FFN_CONV - 1 - j:FFN_CONV - j, cols] * _shift_rows(cat, j, rows)
        carry_ref[:, cols] = up[rows - SUBLANES:, :]
        return y

    acc = x
    acts = []
    nxt = (conv_up(FFN_SPLITS[0][0], FFN_SPLITS[0][1]),
           conv_up(FFN_HIDDEN + FFN_SPLITS[0][0], FFN_SPLITS[0][1]))
    for c, (col, width) in enumerate(FFN_SPLITS):
        gate, val = nxt
        if c + 1 < len(FFN_SPLITS):
            ncol, nwidth = FFN_SPLITS[c + 1]
            nxt = (conv_up(ncol, nwidth), conv_up(FFN_HIDDEN + ncol, nwidth))
        acts.append((jax.nn.silu(gate) * val).astype(BF16))
    acc = acc + jnp.dot(jnp.concatenate(acts, axis=1), wdn_ref[...], preferred_element_type=F32)
    o_ref[...] = _rms_norm(acc, fg_ref[...])


def _ffn(x2d, consts, batch, seq):
    m = x2d.shape[0]
    t = FFN_ROWS
    tiles = seq // t
    rows = pl.BlockSpec((t, D_MODEL), lambda b, s: (b * tiles + s, 0))
    return pl.pallas_call(
        _ffn_kernel,
        grid=(batch, tiles),
        in_specs=[rows] + [_const_spec(c.shape) for c in consts],
        out_specs=rows,
        out_shape=jax.ShapeDtypeStruct((m, D_MODEL), F32),
        scratch_shapes=[pltpu.VMEM((SUBLANES, 2 * FFN_HIDDEN), F32)],
        compiler_params=pltpu.CompilerParams(
            dimension_semantics=("arbitrary", "arbitrary"), vmem_limit_bytes=V7X_VMEM_LIMIT),
        name="convffn",
    )(x2d, *consts)


def _retention_decays():
    heads = jnp.arange(RET_HEADS, dtype=F32)
    log_g = jnp.log1p(-jnp.exp2(-5.0 - heads))
    pos = jnp.arange(RET_CHUNK, dtype=F32)
    rel = pos[:, None] - pos[None, :]
    d_in = jnp.where(rel >= 0, jnp.exp(log_g[:, None, None] * jnp.maximum(rel, 0.0)), 0.0)
    d_q = jnp.exp(log_g[:, None] * (pos + 1.0))
    d_k = jnp.exp(log_g[:, None] * (RET_CHUNK - 1.0 - pos))
    d_chunk = jnp.exp(log_g * RET_CHUNK)
    d_q = jnp.broadcast_to(d_q[:, :, None], (RET_HEADS, RET_CHUNK, RET_V_DIM))
    d_k = jnp.broadcast_to(d_k[:, :, None], (RET_HEADS, RET_CHUNK, RET_QK_DIM))
    return d_chunk, d_in, d_q, d_k


def kernel(x, in_norm_g, w_in, ret_gn_g, w_ret_o, lru_conv_w, lru_conv_b, lru_w_a, lru_b_a,
           lru_w_i, lru_b_i, lru_a_param, w_rnn_o, w_out, ffn_norm_g, w_up, ffn_conv_w,
           ffn_conv_b, w_down, final_norm_g):
    batch, seq, d = x.shape
    depth = w_in.shape[0]
    assert d == D_MODEL and seq % INPROJ_ROWS == 0 and seq % MIXER_ROWS == 0 and seq % FFN_ROWS == 0
    assert depth == 1, "the final norm is fused into the (single) layer's FFN kernel"

    pos = jnp.arange(seq, dtype=F32)
    inv_freq = ROPE_BASE ** (-jnp.arange(0, RET_QK_DIM, 2, dtype=F32) / RET_QK_DIM)
    ang = pos[:, None] * inv_freq[None, :]
    cos = jnp.cos(ang)
    sin = jnp.sin(ang)
    d_chunk, d_in, d_q, d_k = _retention_decays()

    x2d = x.reshape(batch * seq, d)
    for l in range(depth):
        inproj_consts = (
            in_norm_g[l][None, :], w_in[l].astype(BF16), cos, sin,
            lru_conv_w[l], lru_conv_b[l][None, :],
            lru_w_a[l].astype(BF16), lru_b_a[l].reshape(1, LRU_WIDTH),
            lru_w_i[l].astype(BF16), lru_b_i[l].reshape(1, LRU_WIDTH),
            lru_a_param[l][None, :],
        )
        proj = _inproj(x2d, inproj_consts, batch, seq)
        mixer_consts = (
            d_chunk, d_in, d_q, d_k,
            ret_gn_g[l][None, :], w_ret_o[l].astype(BF16),
            w_rnn_o[l].astype(BF16), w_out[l].astype(BF16),
        )
        x2d = _mixer(proj, x2d, mixer_consts, batch, seq)
        ffn_consts = (
            ffn_norm_g[l][None, :], w_up[l].astype(BF16), ffn_conv_w[l],
            ffn_conv_b[l][None, :], w_down[l].astype(BF16),
            final_norm_g[None, :],
        )
        x2d = _ffn(x2d, ffn_consts, batch, seq)
    return x2d.reshape(batch, seq, d)
```

```python
import functools
import math

import jax
import jax.numpy as jnp
from jax import lax
from jax.experimental import pallas as pl
from jax.experimental.pallas import tpu as pltpu

F32 = jnp.float32
BF16 = jnp.bfloat16

D_MODEL = 1024
RET_HEADS = 4
RET_QK_DIM = 256
RET_V_DIM = 512
RET_QK_WIDTH = RET_HEADS * RET_QK_DIM
RET_V_WIDTH = RET_HEADS * RET_V_DIM
RET_CHUNK = 256
ROPE_BASE = 10000.0
LRU_WIDTH = 1024
LRU_BLOCK = 256
LRU_BLOCKS = 4
LRU_CONV = 4
LRU_C = 8.0
FFN_HIDDEN = 2816
FFN_CONV = 3
NORM_EPS = 1e-6
GN_EPS = 1e-5
GELU_A = 2.0 * (2.0 / math.pi) ** 0.5
GELU_B = 0.044715 * GELU_A
TINY = 1e-30

SUBLANES = 8
LANES = 128
V7X_VMEM_LIMIT = 60 * 1024 * 1024

COL_Q = 0
COL_K = COL_Q + RET_QK_WIDTH
COL_V = COL_K + RET_QK_WIDTH
COL_G = COL_V + RET_V_WIDTH
COL_XR = COL_G + RET_V_WIDTH
COL_GR = COL_XR + LRU_WIDTH
COL_MRET = COL_GR + LRU_WIDTH
COL_MRNN = COL_MRET + D_MODEL
IN_WIDTH = COL_MRNN + D_MODEL

INPROJ_ROWS = 512
MM_COLS = 256
TAIL_ROWS = 128
PIECES_PER_BLOCK = 8
FENCE_LAG = 1
MIXER_ROWS = 512
FFN_ROWS = 512
FFN_SPLITS = ((0, 768), (768, 768), (1536, 768), (2304, 512))


def _const_spec(shape):
    nd = len(shape)
    return pl.BlockSpec(shape, lambda *_: (0,) * nd, pipeline_mode=pl.Buffered(1))


def _rms_norm(x, g):
    ms = jnp.mean(x * x, axis=-1, keepdims=True)
    return x * lax.rsqrt(ms + NORM_EPS) * g


def _gelu_tanh(x):
    two_z = x * (GELU_A + GELU_B * (x * x))
    return x / (1.0 + jnp.exp(-two_z))


def _shift_rows(cat, j, rows):
    return pltpu.roll(cat, j, axis=0)[SUBLANES:SUBLANES + rows, :]


def _inproj_kernel(x_ref, g_ref, w_ref, cos_ref, sin_ref,
                   cw_ref, cb_ref, wa_ref, ba_ref, bi_ref, ap_ref,
                   q_ref, k_ref, v_ref, sg_ref, hg_ref, sr_ref, sn_ref,
                   xcarry_ref, hcarry_ref, pre_r_sc, pre_i_sc, xc_sc, gg_sc, h_sc,
                   *, tiles_per_seq, num_tiles):
    rows = INPROJ_ROWS
    t = pl.program_id(0)

    @pl.when(t == 0)
    def _():
        pre_r_sc[...] = jnp.zeros_like(pre_r_sc)
        pre_i_sc[...] = jnp.zeros_like(pre_i_sc)
        xc_sc[...] = jnp.zeros_like(xc_sc)
        gg_sc[...] = jnp.zeros_like(gg_sc)

    @pl.when(jnp.minimum(t, num_tiles - 1) % tiles_per_seq == 0)
    def _():
        xcarry_ref[...] = jnp.zeros_like(xcarry_ref)

    @pl.when(jnp.maximum(t - 1, 0) % tiles_per_seq == 0)
    def _():
        hcarry_ref[...] = jnp.zeros_like(hcarry_ref)

    neg_c_softplus = -LRU_C * jax.nn.softplus(-ap_ref[...])
    first_row = lax.broadcasted_iota(jnp.int32, (SUBLANES, LRU_BLOCK), 0) == 0

    h_carry = [None] * LRU_BLOCKS

    def lru_tail(n, rc):
        cols = slice(n * LRU_BLOCK, (n + 1) * LRU_BLOCK)
        rs = slice(rc * TAIL_ROWS, (rc + 1) * TAIL_ROWS)
        r = jax.nn.sigmoid(pre_r_sc[rs, cols] + ba_ref[:, cols])
        gi = jax.nn.sigmoid(pre_i_sc[rs, cols] + bi_ref[:, cols])
        a_all = jnp.exp(neg_c_softplus[:, cols] * r)
        y = 1.0 - a_all * a_all
        u_all = y * lax.rsqrt(jnp.maximum(y, TINY)) * gi * xc_sc[rs, cols]
        h_prev = hcarry_ref[:, cols] if rc == 0 else h_carry[n]
        for j in range(TAIL_ROWS // SUBLANES):
            grp = slice(j * SUBLANES, (j + 1) * SUBLANES)
            a = a_all[grp, :]
            u = u_all[grp, :] + a * h_prev
            a = jnp.where(first_row, 0.0, a)
            for d in (1, 2):
                u = u + a * pltpu.roll(u, d, axis=0)
                a = a * pltpu.roll(a, d, axis=0)
            hj = u + a * pltpu.roll(u, 4, axis=0)
            out_rows = slice(rc * TAIL_ROWS + j * SUBLANES, rc * TAIL_ROWS + (j + 1) * SUBLANES)
            hg_ref[out_rows, cols] = (hj * gg_sc[out_rows, cols]).astype(BF16)
            h_prev = jnp.where(first_row, pltpu.roll(hj, 1, axis=0), 0.0)
        h_carry[n] = h_prev
        if (rc + 1) * TAIL_ROWS == rows:
            hcarry_ref[:, cols] = h_prev

    h_sc[0:rows, :] = _rms_norm(x_ref[...], g_ref[...]).astype(BF16)
    cos = cos_ref[...]
    sin = sin_ref[...]

    fence_row = rows + lax.shift_right_arithmetic(t, 31)

    def fence(hp):
        mark = jnp.concatenate([hp[:, :LANES], hp[:, LANES:2 * LANES]], axis=0).astype(BF16)
        h_sc[pl.ds(pl.multiple_of(fence_row, 2 * SUBLANES), 2 * SUBLANES), 0:LANES] = mark

    def mm(col):
        return jnp.dot(h_sc[0:rows, :], w_ref[:, col:col + MM_COLS], preferred_element_type=F32)

    cos_k = cos * RET_QK_DIM ** -0.5
    sin_k = sin * RET_QK_DIM ** -0.5

    def rotary(acc, out_ref, lo, c, s):
        half = RET_QK_DIM // 2
        t1 = acc[:, :half]
        t2 = acc[:, half:]
        out_ref[:, lo:lo + half] = (t1 * c - t2 * s).astype(BF16)
        out_ref[:, lo + half:lo + RET_QK_DIM] = (t1 * s + t2 * c).astype(BF16)

    def section(kind, i):
        lo = i * MM_COLS
        if kind == "q":
            rotary(mm(COL_Q + lo), q_ref, lo, cos, sin)
        elif kind == "k":
            rotary(mm(COL_K + lo), k_ref, lo, cos_k, sin_k)
        elif kind == "v":
            v_ref[:, lo:lo + MM_COLS] = mm(COL_V + lo).astype(BF16)
        elif kind == "g":
            sg_ref[:, lo:lo + MM_COLS] = jax.nn.silu(mm(COL_G + lo)).astype(BF16)
        elif kind == "mret":
            sr_ref[:, lo:lo + MM_COLS] = jax.nn.sigmoid(mm(COL_MRET + lo)).astype(BF16)
        else:
            sn_ref[:, lo:lo + MM_COLS] = jax.nn.sigmoid(mm(COL_MRNN + lo)).astype(BF16)

    def lru_front_a(n):
        cols = slice(n * LRU_BLOCK, (n + 1) * LRU_BLOCK)
        xr = mm(COL_XR + n * LRU_BLOCK)
        cat = jnp.concatenate([xcarry_ref[:, cols], xr], axis=0)
        xc = cb_ref[:, cols] + cw_ref[LRU_CONV - 1:LRU_CONV, cols] * xr
        for j in range(1, LRU_CONV):
            xc = xc + cw_ref[LRU_CONV - 1 - j:LRU_CONV - j, cols] * _shift_rows(cat, j, rows)
        xcarry_ref[:, cols] = xr[rows - SUBLANES:, :]
        xc_sc[:, cols] = xc
        return xc.astype(BF16)

    def lru_front_b(n, xcb):
        cols = slice(n * LRU_BLOCK, (n + 1) * LRU_BLOCK)
        gg_sc[:, cols] = _gelu_tanh(mm(COL_GR + n * LRU_BLOCK))
        pre = jnp.dot(xcb, wa_ref[n], preferred_element_type=F32)
        pre_r_sc[:, cols] = pre[:, :LRU_BLOCK]
        pre_i_sc[:, cols] = pre[:, LRU_BLOCK:]

    tail_chunks = rows // TAIL_ROWS
    others = ([("v", i) for i in range(RET_V_WIDTH // MM_COLS)]
              + [("q", i) for i in range(RET_HEADS)] + [("k", i) for i in range(RET_HEADS)]
              + [("g", i) for i in range(RET_V_WIDTH // MM_COLS)]
              + [("mret", i) for i in range(D_MODEL // MM_COLS)]
              + [("mrnn", i) for i in range(D_MODEL // MM_COLS)])
    others.reverse()
    pending = []
    for n in range(LRU_BLOCKS):
        pieces = PIECES_PER_BLOCK if n == 0 else PIECES_PER_BLOCK - 2
        issued = 0
        for rc in range(tail_chunks):
            lru_tail(n, rc)
            pending.append(h_carry[n])
            if len(pending) > FENCE_LAG:
                fence(pending.pop(0))
            due = ((rc + 1) * pieces) // tail_chunks
            while issued < due:
                section(*others.pop())
                issued += 1
        xcb = lru_front_a(n)
        section(*others.pop())
        lru_front_b(n, xcb)
    while others:
        section(*others.pop())


def _inproj(x2d, consts, batch, seq):
    m = x2d.shape[0]
    tm = INPROJ_ROWS
    tiles = seq // tm
    num_tiles = m // tm

    def cur(width):
        return pl.BlockSpec((tm, width), lambda t: (jnp.minimum(t, num_tiles - 1), 0))

    def table():
        return pl.BlockSpec((tm, LANES), lambda t: (jnp.minimum(t, num_tiles - 1) % tiles, 0))

    prev = pl.BlockSpec((tm, LRU_WIDTH), lambda t: (jnp.maximum(t - 1, 0), 0))

    g, w_in, cos, sin = consts[:4]
    lru_consts = consts[4:]
    out_widths = (RET_QK_WIDTH, RET_QK_WIDTH, RET_V_WIDTH, RET_V_WIDTH,
                  LRU_WIDTH, D_MODEL, D_MODEL)
    out_specs = [cur(w) for w in out_widths]
    out_specs[4] = prev
    staged = pltpu.VMEM((tm, LRU_WIDTH), F32)
    return pl.pallas_call(
        functools.partial(_inproj_kernel, tiles_per_seq=tiles, num_tiles=num_tiles),
        grid=(num_tiles + 1,),
        in_specs=[
            cur(D_MODEL),
            _const_spec(g.shape),
            _const_spec(w_in.shape),
            table(),
            table(),
        ] + [_const_spec(c.shape) for c in lru_consts],
        out_specs=out_specs,
        out_shape=[jax.ShapeDtypeStruct((m, w), BF16) for w in out_widths],
        scratch_shapes=[
            pltpu.VMEM((SUBLANES, LRU_WIDTH), F32),
            pltpu.VMEM((SUBLANES, LRU_WIDTH), F32),
            staged, staged, staged, staged,
            pltpu.VMEM((tm + 2 * SUBLANES, D_MODEL), BF16),
        ],
        compiler_params=pltpu.CompilerParams(
            dimension_semantics=("arbitrary",), vmem_limit_bytes=V7X_VMEM_LIMIT),
        name="inproj",
    )(x2d, g, w_in, cos, sin, *lru_consts)


def _mixer_kernel(q_ref, k_ref, v_ref, sg_ref, hg_ref, sr_ref, sn_ref, x_ref,
                  dch_ref, din_ref, dq_ref, dk_ref, gn_ref, wro_ref, wrn_ref, wout_ref,
                  o_ref,
                  state_ref, go_ref):
    rows = MIXER_ROWS

    @pl.when(pl.program_id(1) == 0)
    def _():
        state_ref[...] = jnp.zeros_like(state_ref)

    y_rnn = jnp.dot(hg_ref[...], wrn_ref[...], preferred_element_type=F32)

    for c in range(rows // RET_CHUNK):
        r0 = c * RET_CHUNK
        for hd in range(RET_HEADS):
            qk = slice(hd * RET_QK_DIM, (hd + 1) * RET_QK_DIM)
            vv = slice(hd * RET_V_DIM, (hd + 1) * RET_V_DIM)
            qh = q_ref[r0:r0 + RET_CHUNK, qk]
            kh = k_ref[r0:r0 + RET_CHUNK, qk]
            vh = v_ref[r0:r0 + RET_CHUNK, vv]
            scores = lax.dot_general(qh, kh, (((1,), (1,)), ((), ())),
                                     preferred_element_type=F32)
            scores = (scores * din_ref[hd]).astype(BF16)
            st = state_ref[hd]
            o = (jnp.dot(scores, vh, preferred_element_type=F32)
                 + jnp.dot(qh, st.astype(BF16), preferred_element_type=F32) * dq_ref[hd])
            kd = (kh.astype(F32) * dk_ref[hd]).astype(BF16)
            state_ref[hd] = st * dch_ref[hd] + lax.dot_general(
                kd, vh, (((0,), (0,)), ((), ())), preferred_element_type=F32)
            mu = jnp.mean(o, axis=-1, keepdims=True)
            dev = o - mu
            var = jnp.mean(dev * dev, axis=-1, keepdims=True)
            y = dev * lax.rsqrt(var + GN_EPS) * gn_ref[:, vv]
            go_ref[r0:r0 + RET_CHUNK, vv] = (
                y * sg_ref[r0:r0 + RET_CHUNK, vv].astype(F32)).astype(BF16)
    y_ret = jnp.dot(go_ref[...], wro_ref[...], preferred_element_type=F32)

    mixed = (sr_ref[...].astype(F32) * y_ret + sn_ref[...].astype(F32) * y_rnn).astype(BF16)
    o_ref[...] = x_ref[...] + jnp.dot(mixed, wout_ref[...], preferred_element_type=F32)


def _mixer(proj, x2d, consts, batch, seq):
    q, k, v, sg, hg, sr, sn = proj
    m = x2d.shape[0]
    t = MIXER_ROWS
    tiles = seq // t

    def rows(width):
        return pl.BlockSpec((t, width), lambda b, s: (b * tiles + s, 0))

    const_specs = [pl.BlockSpec(memory_space=pltpu.SMEM)]
    const_specs += [_const_spec(c.shape) for c in consts[1:]]
    return pl.pallas_call(
        _mixer_kernel,
        grid=(batch, tiles),
        in_specs=[rows(RET_QK_WIDTH), rows(RET_QK_WIDTH), rows(RET_V_WIDTH), rows(RET_V_WIDTH),
                  rows(LRU_WIDTH), rows(D_MODEL), rows(D_MODEL), rows(D_MODEL)]
                 + const_specs,
        out_specs=rows(D_MODEL),
        out_shape=jax.ShapeDtypeStruct((m, D_MODEL), F32),
        scratch_shapes=[
            pltpu.VMEM((RET_HEADS, RET_QK_DIM, RET_V_DIM), F32),
            pltpu.VMEM((t, RET_V_WIDTH), BF16),
        ],
        compiler_params=pltpu.CompilerParams(
            dimension_semantics=("arbitrary", "arbitrary"), vmem_limit_bytes=V7X_VMEM_LIMIT),
        name="mixer",
    )(q, k, v, sg, hg, sr, sn, x2d, *consts)


def _ffn_kernel(x_ref, g_ref, wup_ref, cw_ref, cb_ref, wdn_ref, fg_ref, o_ref, carry_ref):
    rows = FFN_ROWS

    @pl.when(pl.program_id(1) == 0)
    def _():
        carry_ref[...] = jnp.zeros_like(carry_ref)

    x = x_ref[...]
    h = _rms_norm(x, g_ref[...]).astype(BF16)

    def conv_up(col, width):
        cols = slice(col, col + width)
        up = jnp.dot(h, wup_ref[:, cols], preferred_element_type=F32)
        cat = jnp.concatenate([carry_ref[:, cols], up], axis=0)
        y = cb_ref[:, cols] + cw_ref[FFN_CONV - 1:FFN_CONV, cols] * up
        for j in range(1, FFN_CONV):
            y = y + cw_ref[FFN_CONV - 1 - j:FFN_CONV - j, cols] * _shift_rows(cat, j, rows)
        carry_ref[:, cols] = up[rows - SUBLANES:, :]
        return y

    acc = x
    acts = []
    nxt = (conv_up(FFN_SPLITS[0][0], FFN_SPLITS[0][1]),
           conv_up(FFN_HIDDEN + FFN_SPLITS[0][0], FFN_SPLITS[0][1]))
    for c, (col, width) in enumerate(FFN_SPLITS):
        gate, val = nxt
        if c + 1 < len(FFN_SPLITS):
            ncol, nwidth = FFN_SPLITS[c + 1]
            nxt = (conv_up(ncol, nwidth), conv_up(FFN_HIDDEN + ncol, nwidth))
        acts.append((jax.nn.silu(gate) * val).astype(BF16))
    acc = acc + jnp.dot(jnp.concatenate(acts, axis=1), wdn_ref[...], preferred_element_type=F32)
    o_ref[...] = _rms_norm(acc, fg_ref[...])


def _ffn(x2d, consts, batch, seq):
    m = x2d.shape[0]
    t = FFN_ROWS
    tiles = seq // t
    rows = pl.BlockSpec((t, D_MODEL), lambda b, s: (b * tiles + s, 0))
    return pl.pallas_call(
        _ffn_kernel,
        grid=(batch, tiles),
        in_specs=[rows] + [_const_spec(c.shape) for c in consts],
        out_specs=rows,
        out_shape=jax.ShapeDtypeStruct((m, D_MODEL), F32),
        scratch_shapes=[pltpu.VMEM((SUBLANES, 2 * FFN_HIDDEN), F32)],
        compiler_params=pltpu.CompilerParams(
            dimension_semantics=("arbitrary", "arbitrary"), vmem_limit_bytes=V7X_VMEM_LIMIT),
        name="convffn",
    )(x2d, *consts)


def _retention_decays():
    heads = jnp.arange(RET_HEADS, dtype=F32)
    log_g = jnp.log1p(-jnp.exp2(-5.0 - heads))
    pos = jnp.arange(RET_CHUNK, dtype=F32)
    rel = pos[:, None] - pos[None, :]
    d_in = jnp.where(rel >= 0, jnp.exp(log_g[:, None, None] * jnp.maximum(rel, 0.0)), 0.0)
    d_q = jnp.exp(log_g[:, None] * (pos + 1.0))
    d_k = jnp.exp(log_g[:, None] * (RET_CHUNK - 1.0 - pos))
    d_chunk = jnp.exp(log_g * RET_CHUNK)
    d_q = jnp.broadcast_to(d_q[:, :, None], (RET_HEADS, RET_CHUNK, RET_V_DIM))
    d_k = jnp.broadcast_to(d_k[:, :, None], (RET_HEADS, RET_CHUNK, RET_QK_DIM))
    return d_chunk, d_in, d_q, d_k


def kernel(x, in_norm_g, w_in, ret_gn_g, w_ret_o, lru_conv_w, lru_conv_b, lru_w_a, lru_b_a,
           lru_w_i, lru_b_i, lru_a_param, w_rnn_o, w_out, ffn_norm_g, w_up, ffn_conv_w,
           ffn_conv_b, w_down, final_norm_g):
    batch, seq, d = x.shape
    depth = w_in.shape[0]
    assert d == D_MODEL and seq % INPROJ_ROWS == 0 and seq % MIXER_ROWS == 0 and seq % FFN_ROWS == 0
    assert depth == 1, "the final norm is fused into the (single) layer's FFN kernel"

    pos = jnp.arange(seq, dtype=F32)
    inv_freq = ROPE_BASE ** (-jnp.arange(0, RET_QK_DIM, 2, dtype=F32) / RET_QK_DIM)
    ang = pos[:, None] * inv_freq[None, :]
    cos = jnp.cos(ang)
    sin = jnp.sin(ang)
    d_chunk, d_in, d_q, d_k = _retention_decays()

    x2d = x.reshape(batch * seq, d)
    for l in range(depth):
        inproj_consts = (
            in_norm_g[l][None, :], w_in[l].astype(BF16), cos, sin,
            lru_conv_w[l], lru_conv_b[l][None, :],
            jnp.concatenate([lru_w_a[l], lru_w_i[l]], axis=-1).astype(BF16),
            lru_b_a[l].reshape(1, LRU_WIDTH),
            lru_b_i[l].reshape(1, LRU_WIDTH),
            lru_a_param[l][None, :],
        )
        proj = _inproj(x2d, inproj_consts, batch, seq)
        mixer_consts = (
            d_chunk, d_in, d_q, d_k,
            ret_gn_g[l][None, :], w_ret_o[l].astype(BF16),
            w_rnn_o[l].astype(BF16), w_out[l].astype(BF16),
        )
        x2d = _mixer(proj, x2d, mixer_consts, batch, seq)
        ffn_consts = (
            ffn_norm_g[l][None, :], w_up[l].astype(BF16), ffn_conv_w[l],
            ffn_conv_b[l][None, :], w_down[l].astype(BF16),
            final_norm_g[None, :],
        )
        x2d = _ffn(x2d, ffn_consts, batch, seq)
    return x2d.reshape(batch, seq, d)
```
